```python
import jax, jax.numpy as jnp
from jax import lax
import numpy as np

D_MODEL = 1024
BATCH = 8
SEQ = 2048
DEPTH = 1

D_MIX = D_MODEL
CONV_GROUPS = 8
CONV_DIM = D_MIX // 2
CONV_KSIZE = 3
SB_HEADS = 8
SB_HEAD_DIM = 64
SB_DIM = SB_HEADS * SB_HEAD_DIM
Q_BLOCK = 128
D_FF = 4 * D_MODEL
N_MOD = 6
EPS = 1e-6
IN_SPLITS = (CONV_DIM, 2 * CONV_DIM, 3 * CONV_DIM,
             3 * CONV_DIM + SB_DIM, 3 * CONV_DIM + 2 * SB_DIM)
D_IN = 3 * CONV_DIM + 3 * SB_DIM

kernel_name = "hybrid_shortconv_stickbreaking_block"


def rms_norm(x, g):
    xf = x.astype(jnp.float32)
    y = xf * lax.rsqrt(jnp.mean(xf * xf, axis=-1, keepdims=True) + EPS)
    return (y * g.astype(jnp.float32)).astype(x.dtype)


def modulate(h, shift, scale):
    return h * (1 + scale[:, None, :]) + shift[:, None, :]


def causal_depthwise_conv(u, w):
    return lax.conv_general_dilated(
        u, w[:, None, :].astype(u.dtype), window_strides=(1,),
        padding=[(CONV_KSIZE - 1, 0)],
        dimension_numbers=("NWC", "WIO", "NWC"),
        feature_group_count=u.shape[-1])


def stick_breaking_attention(q, k, v):
    S = q.shape[1]
    scale = SB_HEAD_DIM ** -0.5
    outs = []
    for i in range(S // Q_BLOCK):
        t0, t1 = i * Q_BLOCK, (i + 1) * Q_BLOCK
        qb = q[:, t0:t1].astype(jnp.float32)
        kb = k[:, :t1].astype(jnp.float32)
        vb = v[:, :t1].astype(jnp.float32)
        z = jnp.einsum("bthd,bshd->bhts", qb, kb) * scale
        t_idx = t0 + jnp.arange(Q_BLOCK)[:, None]
        s_idx = jnp.arange(t1)[None, :]
        causal = s_idx < t_idx
        log_beta = jax.nn.log_sigmoid(z)
        log_1mb = jnp.where(causal, log_beta - z, 0.0)
        rc = lax.cumsum(log_1mb, axis=3, reverse=True)
        log_rem = jnp.concatenate(
            [rc[..., 1:], jnp.zeros_like(rc[..., :1])], axis=-1)
        a = jnp.where(causal, jnp.exp(log_beta + log_rem), 0.0)
        o = jnp.einsum("bhts,bshd->bthd", a, vb)
        outs.append(o.astype(v.dtype))
    return jnp.concatenate(outs, axis=1)


def setup_inputs(seed: int = 0) -> dict:
    key = jax.random.key(seed)
    ks = jax.random.split(key, 20)
    f32 = jnp.float32
    nrm = lambda k, shape, s: jax.random.normal(k, shape, f32) * s
    return {
        "x": nrm(ks[0], (BATCH, SEQ, D_MODEL), 1.0),
        "c": nrm(ks[1], (BATCH, D_MODEL), 1.0),
        "w_ada": nrm(ks[2], (D_MODEL, N_MOD * D_MODEL), D_MODEL ** -0.5),
        "b_ada": nrm(ks[3], (N_MOD * D_MODEL,), 0.01),
        "norm1_g": 1.0 + nrm(ks[4], (D_MODEL,), 0.02),
        "w_in": nrm(ks[5], (D_MODEL, D_IN), D_MODEL ** -0.5),
        "conv_w": nrm(ks[6], (CONV_KSIZE, CONV_DIM), CONV_KSIZE ** -0.5),
        "q_norm_g": 1.0 + nrm(ks[7], (SB_HEAD_DIM,), 0.02),
        "k_norm_g": 1.0 + nrm(ks[8], (SB_HEAD_DIM,), 0.02),
        "conv_out_g": 1.0 + nrm(ks[9], (CONV_DIM,), 0.02),
        "attn_out_g": 1.0 + nrm(ks[10], (SB_DIM,), 0.02),
        "w_out": nrm(ks[11], (D_MIX, D_MODEL), D_MIX ** -0.5),
        "norm2_g": 1.0 + nrm(ks[12], (D_MODEL,), 0.02),
        "w_ff1": nrm(ks[13], (D_MODEL, D_FF), D_MODEL ** -0.5),
        "w_ff2": nrm(ks[14], (D_FF, D_MODEL), D_FF ** -0.5),
    }


def reference(x, c, w_ada, b_ada, norm1_g, w_in, conv_w, q_norm_g, k_norm_g,
              conv_out_g, attn_out_g, w_out, norm2_g, w_ff1, w_ff2):
    B, S, D = x.shape
    mod = jax.nn.silu(c) @ w_ada + b_ada
    shift1, scale1, gate1, shift2, scale2, gate2 = jnp.split(mod, N_MOD, axis=-1)

    for _ in range(DEPTH):
        h = modulate(rms_norm(x, norm1_g), shift1, scale1)
        proj = h @ w_in
        b_gate, c_gate, u, q, k, v = jnp.split(proj, IN_SPLITS, axis=-1)

        y_conv = b_gate * causal_depthwise_conv(c_gate * u, conv_w)

        q = rms_norm(q.reshape(B, S, SB_HEADS, SB_HEAD_DIM), q_norm_g)
        k = rms_norm(k.reshape(B, S, SB_HEADS, SB_HEAD_DIM), k_norm_g)
        v = v.reshape(B, S, SB_HEADS, SB_HEAD_DIM)
        y_attn = stick_breaking_attention(q, k, v).reshape(B, S, SB_DIM)

        mix = jnp.concatenate(
            [rms_norm(y_conv, conv_out_g), rms_norm(y_attn, attn_out_g)], axis=-1)
        x = x + gate1[:, None, :] * (mix @ w_out)

        h2 = modulate(rms_norm(x, norm2_g), shift2, scale2)
        f = jnp.square(jax.nn.relu(h2 @ w_ff1)) @ w_ff2
        x = x + gate2[:, None, :] * f
    return x
```

```python
import functools

import jax
import jax.numpy as jnp
from jax import lax
from jax.experimental import pallas as pl
from jax.experimental.pallas import tpu as pltpu

EPS = 1e-6
CONV_DIM = 512
SB_DIM = 512
SB_HEAD_DIM = 64
N_MOD = 6
LANES = 128
ATT_TILE = 128
ROW_TILE = 512
VMEM_LIMIT = 56 * 1024 * 1024

F32 = jnp.float32
BF16 = jnp.bfloat16


def _split_bf16(a):
    hi = a.astype(BF16)
    lo = (a - hi.astype(F32)).astype(BF16)
    return hi, lo


def _const_spec(shape):
    return pl.BlockSpec(shape, lambda *_: (0,) * len(shape), pipeline_mode=pl.Buffered(1))


def _adaln_kernel(c_ref, w_ref, b_ref, o_ref):
    c = c_ref[...]
    s = c * (1.0 / (1.0 + jnp.exp(-c)))
    s_hi, s_lo = _split_bf16(s)
    w_hi, w_lo = _split_bf16(w_ref[...])
    dot = functools.partial(jnp.dot, preferred_element_type=F32)
    o_ref[...] = dot(s_hi, w_hi) + (dot(s_hi, w_lo) + dot(s_lo, w_hi)) + b_ref[...]


def _adaln(c, w_ada, b_ada):
    bsz, d = c.shape
    n = w_ada.shape[1]
    tn = 1024
    return pl.pallas_call(
        _adaln_kernel,
        grid=(n // tn,),
        in_specs=[pl.BlockSpec((bsz, d), lambda j: (0, 0)),
                  pl.BlockSpec((d, tn), lambda j: (0, j)),
                  pl.BlockSpec((1, tn), lambda j: (0, j))],
        out_specs=pl.BlockSpec((bsz, tn), lambda j: (0, j)),
        out_shape=jax.ShapeDtypeStruct((bsz, n), F32),
        compiler_params=pltpu.CompilerParams(dimension_semantics=("parallel",),
                                             vmem_limit_bytes=VMEM_LIMIT),
        name="adaln",
    )(c, w_ada, b_ada.reshape(1, n))


def _rms(xf, g):
    ms = jnp.mean(xf * xf, axis=-1, keepdims=True)
    return xf * lax.rsqrt(ms + EPS) * g


def _in_proj_kernel(tiles_per_seq, x_ref, mod_ref, g1_ref, w_ref, cw_ref, cg_ref,
                    mix_ref, qkv_ref, ext_ref):
    tm = x_ref.shape[0]
    i = pl.program_id(0)
    shift = mod_ref[0:1, :]
    scale = mod_ref[1:2, :]
    h = _rms(x_ref[...], g1_ref[...]) * (1.0 + scale) + shift
    hb = h.astype(BF16)

    n_conv = 3 * CONV_DIM
    p = jnp.dot(hb, w_ref[:, :n_conv], preferred_element_type=F32)
    b_gate = p[:, :CONV_DIM]
    cu = p[:, CONV_DIM:2 * CONV_DIM] * p[:, 2 * CONV_DIM:]

    @pl.when(i % tiles_per_seq == 0)
    def _():
        ext_ref[0:8, :] = jnp.zeros((8, CONV_DIM), F32)

    @pl.when(i % tiles_per_seq != 0)
    def _():
        ext_ref[0:8, :] = ext_ref[tm:tm + 8, :]

    ext_ref[8:tm + 8, :] = cu
    conv = (cw_ref[2:3, :] * cu + cw_ref[1:2, :] * ext_ref[7:tm + 7, :]
            + cw_ref[0:1, :] * ext_ref[6:tm + 6, :])
    mix_ref[...] = _rms(b_gate * conv, cg_ref[...]).astype(BF16)

    qkv_ref[...] = jnp.dot(hb, w_ref[:, n_conv:], preferred_element_type=F32)


def _in_proj(x2, mod3, norm1_g, w_in_b, conv_w, conv_out_g, seq):
    rows, d = x2.shape
    tm = ROW_TILE
    tiles_per_seq = seq // tm
    n_in = w_in_b.shape[1]
    n_qkv = n_in - 3 * CONV_DIM
    return pl.pallas_call(
        functools.partial(_in_proj_kernel, tiles_per_seq),
        grid=(rows // tm,),
        in_specs=[pl.BlockSpec((tm, d), lambda i: (i, 0)),
                  pl.BlockSpec((None, N_MOD, d), lambda i: (i // tiles_per_seq, 0, 0)),
                  _const_spec((1, d)),
                  _const_spec((d, n_in)),
                  _const_spec(conv_w.shape),
                  _const_spec((1, CONV_DIM))],
        out_specs=[pl.BlockSpec((tm, CONV_DIM), lambda i: (i, 0)),
                   pl.BlockSpec((tm, n_qkv), lambda i: (i, 0))],
        out_shape=[jax.ShapeDtypeStruct((rows, CONV_DIM), BF16),
                   jax.ShapeDtypeStruct((rows, n_qkv), F32)],
        scratch_shapes=[pltpu.VMEM((tm + 8, CONV_DIM), F32)],
        compiler_params=pltpu.CompilerParams(dimension_semantics=("arbitrary",),
                                             vmem_limit_bytes=VMEM_LIMIT),
        name="in_proj",
    )(x2, mod3, norm1_g.reshape(1, d), w_in_b, conv_w, conv_out_g.reshape(1, CONV_DIM))


def _head_rms(a, g, lo_lane):
    s = a * a
    sum_lo = jnp.sum(jnp.where(lo_lane, s, 0.0), axis=-1, keepdims=True)
    sum_hi = jnp.sum(jnp.where(lo_lane, 0.0, s), axis=-1, keepdims=True)
    ms = jnp.where(lo_lane, sum_lo, sum_hi) * (1.0 / SB_HEAD_DIM)
    return a * lax.rsqrt(ms + EPS) * g


def _sb_attn_kernel(q_ref, k_ref, v_ref, qg_ref, kg_ref, tri_ref, o_ref,
                    kt_ref, vb_ref, acc_ref, carry_ref):
    t = ATT_TILE
    n_tiles = q_ref.shape[0] // t
    lo_lane = lax.broadcasted_iota(jnp.int32, (t, LANES), 1) < SB_HEAD_DIM
    row = lax.broadcasted_iota(jnp.int32, (2 * t, t), 0) & (t - 1)
    col = lax.broadcasted_iota(jnp.int32, (2 * t, t), 1)
    causal = col < row
    tri = tri_ref[...]

    def prep(j, _):
        r0 = pl.multiple_of(j * t, t)
        kn = _head_rms(k_ref[pl.ds(r0, t), :], kg_ref[...], lo_lane)
        kt_ref[j] = kn.T.astype(BF16)
        vb_ref[j] = v_ref[pl.ds(r0, t), :].astype(BF16)
        return 0

    lax.fori_loop(0, n_tiles, prep, 0)

    def tile_step(q2, j, masked):
        z = jnp.dot(q2, kt_ref[j], preferred_element_type=F32)
        sp = jnp.log(1.0 + jnp.exp(-jnp.abs(z)))
        log_beta = jnp.minimum(z, 0.0) - sp
        log_1mb = log_beta - z
        if masked:
            log_1mb = jnp.where(causal, log_1mb, 0.0)
        l_hi, l_lo = _split_bf16(log_1mb)
        cum = jnp.dot(jnp.concatenate([l_hi, l_lo], axis=1), tri,
                      preferred_element_type=F32)
        carry = carry_ref[...]
        a = jnp.exp(log_beta + (carry + cum[:, :t]))
        if masked:
            a = jnp.where(causal, a, 0.0)
        acc_ref[...] += jnp.dot(a.astype(BF16), vb_ref[j], preferred_element_type=F32)
        carry_ref[...] = carry + cum[:, t:]

    def q_tile(qi, _):
        r0 = pl.multiple_of(qi * t, t)
        qn = _head_rms(q_ref[pl.ds(r0, t), :], qg_ref[...], lo_lane) * (SB_HEAD_DIM ** -0.5)
        q2 = jnp.concatenate([jnp.where(lo_lane, qn, 0.0), jnp.where(lo_lane, 0.0, qn)],
                             axis=0).astype(BF16)
        acc_ref[...] = jnp.zeros_like(acc_ref)
        carry_ref[...] = jnp.zeros_like(carry_ref)
        tile_step(q2, qi, True)

        def body(it, _):
            tile_step(q2, qi - 1 - it, False)
            return 0

        lax.fori_loop(0, qi, body, 0)
        acc = acc_ref[...]
        o_ref[pl.ds(r0, t), :] = jnp.where(lo_lane, acc[:t], acc[t:])
        return 0

    lax.fori_loop(0, n_tiles, q_tile, 0)


def _sb_attn(qkv, q_norm_g, k_norm_g, bsz, seq):
    rows = qkv.shape[0]
    t = ATT_TILE
    n_pairs = SB_DIM // LANES
    n_tiles = seq // t
    j_idx = jnp.arange(2 * t)[:, None] % t
    s_idx = jnp.arange(2 * t)[None, :]
    tri = ((s_idx >= t) | (j_idx > s_idx)).astype(BF16)
    reps = LANES // SB_HEAD_DIM
    qg = jnp.tile(q_norm_g, reps).reshape(1, LANES)
    kg = jnp.tile(k_norm_g, reps).reshape(1, LANES)
    return pl.pallas_call(
        _sb_attn_kernel,
        grid=(bsz, n_pairs),
        in_specs=[pl.BlockSpec((seq, LANES), lambda b, p: (b, p)),
                  pl.BlockSpec((seq, LANES), lambda b, p: (b, n_pairs + p)),
                  pl.BlockSpec((seq, LANES), lambda b, p: (b, 2 * n_pairs + p)),
                  _const_spec((1, LANES)),
                  _const_spec((1, LANES)),
                  _const_spec((2 * t, 2 * t))],
        out_specs=pl.BlockSpec((seq, LANES), lambda b, p: (b, p)),
        out_shape=jax.ShapeDtypeStruct((rows, SB_DIM), F32),
        scratch_shapes=[pltpu.VMEM((n_tiles, LANES, t), BF16),
                        pltpu.VMEM((n_tiles, t, LANES), BF16),
                        pltpu.VMEM((2 * t, LANES), F32),
                        pltpu.VMEM((2 * t, t), F32)],
        compiler_params=pltpu.CompilerParams(dimension_semantics=("parallel", "parallel"),
                                             vmem_limit_bytes=VMEM_LIMIT),
        name="sb_attn",
    )(qkv, qkv, qkv, qg, kg, tri)


def _out_proj_kernel(mix_ref, ya_ref, x_ref, mod_ref, ag_ref, w_ref, g2_ref, x1_ref, h2_ref):
    yb = _rms(ya_ref[...], ag_ref[...]).astype(BF16)
    o = (jnp.dot(mix_ref[...], w_ref[:CONV_DIM, :], preferred_element_type=F32)
         + jnp.dot(yb, w_ref[CONV_DIM:, :], preferred_element_type=F32))
    x1 = x_ref[...] + mod_ref[2:3, :] * o
    x1_ref[...] = x1
    h2 = _rms(x1, g2_ref[...]) * (1.0 + mod_ref[4:5, :]) + mod_ref[3:4, :]
    h2_ref[...] = h2.astype(BF16)


def _out_proj(mix_a, y_attn, x2, mod3, attn_out_g, w_out_b, norm2_g, seq):
    rows, d = x2.shape
    tm = ROW_TILE
    tiles_per_seq = seq // tm
    return pl.pallas_call(
        _out_proj_kernel,
        grid=(rows // tm,),
        in_specs=[pl.BlockSpec((tm, CONV_DIM), lambda i: (i, 0)),
                  pl.BlockSpec((tm, SB_DIM), lambda i: (i, 0)),
                  pl.BlockSpec((tm, d), lambda i: (i, 0)),
                  pl.BlockSpec((None, N_MOD, d), lambda i: (i // tiles_per_seq, 0, 0)),
                  _const_spec((1, SB_DIM)),
                  _const_spec(w_out_b.shape),
                  _const_spec((1, d))],
        out_specs=[pl.BlockSpec((tm, d), lambda i: (i, 0)),
                   pl.BlockSpec((tm, d), lambda i: (i, 0))],
        out_shape=[jax.ShapeDtypeStruct((rows, d), F32),
                   jax.ShapeDtypeStruct((rows, d), BF16)],
        compiler_params=pltpu.CompilerParams(dimension_semantics=("parallel",),
                                             vmem_limit_bytes=VMEM_LIMIT),
        name="out_proj",
    )(mix_a, y_attn, x2, mod3, attn_out_g.reshape(1, SB_DIM), w_out_b, norm2_g.reshape(1, d))


def _ffn_kernel(n_chunks, h2_ref, x1_ref, mod_ref, w1_ref, w2_ref, o_ref, acc_ref):
    h2 = h2_ref[...]
    cw = w1_ref.shape[1] // n_chunks
    for c in range(n_chunks):
        u = jnp.dot(h2, w1_ref[:, c * cw:(c + 1) * cw], preferred_element_type=F32)
        r = jnp.maximum(u, 0.0)
        f = jnp.dot((r * r).astype(BF16), w2_ref[c * cw:(c + 1) * cw, :],
                    preferred_element_type=F32)
        if c == 0:
            acc_ref[...] = f
        else:
            acc_ref[...] += f
    o_ref[...] = x1_ref[...] + mod_ref[5:6, :] * acc_ref[...]


def _ffn(h2, x1, mod3, w1_b, w2_b, seq):
    rows, d = x1.shape
    tm = ROW_TILE
    tiles_per_seq = seq // tm
    d_ff = w1_b.shape[1]
    n_chunks = 4
    return pl.pallas_call(
        functools.partial(_ffn_kernel, n_chunks),
        grid=(rows // tm,),
        in_specs=[pl.BlockSpec((tm, d), lambda i: (i, 0)),
                  pl.BlockSpec((tm, d), lambda i: (i, 0)),
                  pl.BlockSpec((None, N_MOD, d), lambda i: (i // tiles_per_seq, 0, 0)),
                  _const_spec((d, d_ff)),
                  _const_spec((d_ff, d))],
        out_specs=pl.BlockSpec((tm, d), lambda i: (i, 0)),
        out_shape=jax.ShapeDtypeStruct((rows, d), F32),
        scratch_shapes=[pltpu.VMEM((tm, d), F32)],
        compiler_params=pltpu.CompilerParams(dimension_semantics=("parallel",),
                                             vmem_limit_bytes=VMEM_LIMIT),
        name="ffn",
    )(h2, x1, mod3, w1_b, w2_b)


def kernel(x, c, w_ada, b_ada, norm1_g, w_in, conv_w, q_norm_g, k_norm_g, conv_out_g,
           attn_out_g, w_out, norm2_g, w_ff1, w_ff2):
    bsz, seq, d = x.shape
    assert seq % ROW_TILE == 0 and seq % ATT_TILE == 0
    assert w_in.shape[1] == 3 * CONV_DIM + 3 * SB_DIM
    x2 = x.reshape(bsz * seq, d)
    mod3 = _adaln(c, w_ada, b_ada).reshape(bsz, N_MOD, d)
    mix_a, qkv = _in_proj(x2, mod3, norm1_g, w_in.astype(BF16), conv_w, conv_out_g, seq)
    y_attn = _sb_attn(qkv, q_norm_g, k_norm_g, bsz, seq)
    x1, h2 = _out_proj(mix_a, y_attn, x2, mod3, attn_out_g, w_out.astype(BF16), norm2_g, seq)
    out = _ffn(h2, x1, mod3, w_ff1.astype(BF16), w_ff2.astype(BF16), seq)
    return out.reshape(bsz, seq, d)
```

```python
import functools

import jax
import jax.numpy as jnp
from jax import lax
from jax.experimental import pallas as pl
from jax.experimental.pallas import tpu as pltpu

EPS = 1e-6
CONV_DIM = 512
SB_DIM = 512
SB_HEAD_DIM = 64
N_MOD = 6
LANES = 128
ATT_TILE = 128
Q_TILE = 512
LOG2E = 1.4426950408889634
ROW_TILE = 512
VMEM_LIMIT = 56 * 1024 * 1024

F32 = jnp.float32
BF16 = jnp.bfloat16


def _split_bf16(a):
    hi = a.astype(BF16)
    lo = (a - hi.astype(F32)).astype(BF16)
    return hi, lo


def _const_spec(shape):
    return pl.BlockSpec(shape, lambda *_: (0,) * len(shape), pipeline_mode=pl.Buffered(1))


def _adaln_kernel(c_ref, w_ref, b_ref, o_ref):
    c = c_ref[...]
    s = c * (1.0 / (1.0 + jnp.exp(-c)))
    s_hi, s_lo = _split_bf16(s)
    w_hi, w_lo = _split_bf16(w_ref[...])
    dot = functools.partial(jnp.dot, preferred_element_type=F32)
    o_ref[...] = dot(s_hi, w_hi) + (dot(s_hi, w_lo) + dot(s_lo, w_hi)) + b_ref[...]


def _adaln(c, w_ada, b_ada):
    bsz, d = c.shape
    n = w_ada.shape[1]
    tn = 1024
    return pl.pallas_call(
        _adaln_kernel,
        grid=(n // tn,),
        in_specs=[pl.BlockSpec((bsz, d), lambda j: (0, 0)),
                  pl.BlockSpec((d, tn), lambda j: (0, j)),
                  pl.BlockSpec((1, tn), lambda j: (0, j))],
        out_specs=pl.BlockSpec((bsz, tn), lambda j: (0, j)),
        out_shape=jax.ShapeDtypeStruct((bsz, n), F32),
        compiler_params=pltpu.CompilerParams(dimension_semantics=("parallel",),
                                             vmem_limit_bytes=VMEM_LIMIT),
        name="adaln",
    )(c, w_ada, b_ada.reshape(1, n))


def _rms(xf, g):
    ms = jnp.mean(xf * xf, axis=-1, keepdims=True)
    return xf * lax.rsqrt(ms + EPS) * g


def _in_proj_kernel(tiles_per_seq, x_ref, mod_ref, g1_ref, w_ref, cw_ref, cg_ref,
                    mix_ref, qkv_ref, ext_ref):
    tm = x_ref.shape[0]
    i = pl.program_id(0)
    shift = mod_ref[0:1, :]
    scale = mod_ref[1:2, :]
    h = _rms(x_ref[...], g1_ref[...]) * (1.0 + scale) + shift
    hb = h.astype(BF16)

    n_conv = 3 * CONV_DIM
    p = jnp.dot(hb, w_ref[:, :n_conv], preferred_element_type=F32)
    b_gate = p[:, :CONV_DIM]
    cu = p[:, CONV_DIM:2 * CONV_DIM] * p[:, 2 * CONV_DIM:]

    @pl.when(i % tiles_per_seq == 0)
    def _():
        ext_ref[0:8, :] = jnp.zeros((8, CONV_DIM), F32)

    @pl.when(i % tiles_per_seq != 0)
    def _():
        ext_ref[0:8, :] = ext_ref[tm:tm + 8, :]

    ext_ref[8:tm + 8, :] = cu
    conv = (cw_ref[2:3, :] * cu + cw_ref[1:2, :] * ext_ref[7:tm + 7, :]
            + cw_ref[0:1, :] * ext_ref[6:tm + 6, :])
    mix_ref[...] = _rms(b_gate * conv, cg_ref[...]).astype(BF16)

    qkv_ref[...] = jnp.dot(hb, w_ref[:, n_conv:], preferred_element_type=F32)


def _in_proj(x2, mod3, norm1_g, w_in_b, conv_w, conv_out_g, seq):
    rows, d = x2.shape
    tm = ROW_TILE
    tiles_per_seq = seq // tm
    n_in = w_in_b.shape[1]
    n_qkv = n_in - 3 * CONV_DIM
    return pl.pallas_call(
        functools.partial(_in_proj_kernel, tiles_per_seq),
        grid=(rows // tm,),
        in_specs=[pl.BlockSpec((tm, d), lambda i: (i, 0)),
                  pl.BlockSpec((None, N_MOD, d), lambda i: (i // tiles_per_seq, 0, 0)),
                  _const_spec((1, d)),
                  _const_spec((d, n_in)),
                  _const_spec(conv_w.shape),
                  _const_spec((1, CONV_DIM))],
        out_specs=[pl.BlockSpec((tm, CONV_DIM), lambda i: (i, 0)),
                   pl.BlockSpec((tm, n_qkv), lambda i: (i, 0))],
        out_shape=[jax.ShapeDtypeStruct((rows, CONV_DIM), BF16),
                   jax.ShapeDtypeStruct((rows, n_qkv), F32)],
        scratch_shapes=[pltpu.VMEM((tm + 8, CONV_DIM), F32)],
        compiler_params=pltpu.CompilerParams(dimension_semantics=("arbitrary",),
                                             vmem_limit_bytes=VMEM_LIMIT),
        name="in_proj",
    )(x2, mod3, norm1_g.reshape(1, d), w_in_b, conv_w, conv_out_g.reshape(1, CONV_DIM))


def _head_rms(a, g, lo_lane):
    s = a * a
    sum_lo = jnp.sum(jnp.where(lo_lane, s, 0.0), axis=-1, keepdims=True)
    sum_hi = jnp.sum(jnp.where(lo_lane, 0.0, s), axis=-1, keepdims=True)
    ms = jnp.where(lo_lane, sum_lo, sum_hi) * (1.0 / SB_HEAD_DIM)
    return a * lax.rsqrt(ms + EPS) * g


def _sb_attn_kernel(q_ref, k_ref, v_ref, qg_ref, kg_ref, tri_ref, o_ref,
                    kt_ref, vb_ref, q2_ref, acc_ref, carry_ref):
    t = ATT_TILE
    sub = Q_TILE // t
    blk = 2 * t
    n_ktiles = k_ref.shape[0] // t
    n_qtiles = q_ref.shape[0] // Q_TILE
    lo_lane = lax.broadcasted_iota(jnp.int32, (t, LANES), 1) < SB_HEAD_DIM
    row = lax.broadcasted_iota(jnp.int32, (blk, t), 0) & (t - 1)
    col = lax.broadcasted_iota(jnp.int32, (blk, t), 1)
    causal = col < row
    tri = tri_ref[...]

    def prep(j, _):
        r0 = pl.multiple_of(j * t, t)
        kn = _head_rms(k_ref[pl.ds(r0, t), :], kg_ref[...], lo_lane)
        kt_ref[j] = kn.T.astype(BF16)
        vb_ref[j] = v_ref[pl.ds(r0, t), :].astype(BF16)
        return 0

    lax.fori_loop(0, n_ktiles, prep, 0)

    def scores(j, r0):
        z = jnp.dot(q2_ref[r0:, :], kt_ref[j], preferred_element_type=F32)
        sp = jnp.log2(1.0 + jnp.exp2(-jnp.abs(z)))
        log_beta = jnp.minimum(z, 0.0) - sp
        return log_beta, log_beta - z

    def cumsum(log_1mb):
        l_hi, l_lo = _split_bf16(log_1mb)
        cum = jnp.dot(jnp.concatenate([l_hi, l_lo], axis=1), tri, preferred_element_type=F32)
        return cum[:, :t], cum[:, t:]

    def full_step(j):
        log_beta, log_1mb = scores(j, 0)
        excl, tot = cumsum(log_1mb)
        carry = carry_ref[...]
        a = jnp.exp2(log_beta + (carry + excl))
        acc_ref[...] += jnp.dot(a.astype(BF16), vb_ref[j], preferred_element_type=F32)
        carry_ref[...] = carry + tot

    def diag_step(j, d):
        r0 = d * blk
        log_beta, log_1mb = scores(j, r0)
        masked = jnp.where(causal, log_1mb[:blk], 0.0)
        if d == sub - 1:
            log_1mb = masked
        else:
            log_1mb = jnp.concatenate([masked, log_1mb[blk:]], axis=0)
        excl, tot = cumsum(log_1mb)
        a_first = jnp.where(causal, jnp.exp2(log_beta[:blk] + excl[:blk]), 0.0)
        if d == sub - 1:
            a = a_first
        else:
            carry = carry_ref[r0 + blk:, :]
            a = jnp.concatenate(
                [a_first, jnp.exp2(log_beta[blk:] + (carry + excl[blk:]))], axis=0)
            carry_ref[r0 + blk:, :] = carry + tot[blk:]
        av = jnp.dot(a.astype(BF16), vb_ref[j], preferred_element_type=F32)
        acc_ref[r0:r0 + blk, :] = av[:blk]
        carry_ref[r0:r0 + blk, :] = tot[:blk]
        if d != sub - 1:
            acc_ref[r0 + blk:, :] += av[blk:]

    def q_tile(qi, _):
        q0 = pl.multiple_of(qi * Q_TILE, Q_TILE)
        qn = _head_rms(q_ref[pl.ds(q0, Q_TILE), :], qg_ref[...],
                       lax.broadcasted_iota(jnp.int32, (Q_TILE, LANES), 1) < SB_HEAD_DIM)
        qn = qn * (SB_HEAD_DIM ** -0.5 * LOG2E)
        for r in range(sub):
            qr = qn[r * t:(r + 1) * t]
            q2_ref[r * blk:r * blk + t, :] = jnp.where(lo_lane, qr, 0.0).astype(BF16)
            q2_ref[r * blk + t:(r + 1) * blk, :] = jnp.where(lo_lane, 0.0, qr).astype(BF16)
        for d in reversed(range(sub)):
            diag_step(qi * sub + d, d)

        def body(it, _):
            full_step(qi * sub - 1 - it)
            return 0

        lax.fori_loop(0, qi * sub, body, 0)
        for r in range(sub):
            o_ref[pl.ds(q0 + r * t, t), :] = jnp.where(
                lo_lane, acc_ref[r * blk:r * blk + t, :], acc_ref[r * blk + t:(r + 1) * blk, :])
        return 0

    lax.fori_loop(0, n_qtiles, q_tile, 0)


def _sb_attn(qkv, q_norm_g, k_norm_g, bsz, seq):
    rows = qkv.shape[0]
    t = ATT_TILE
    n_pairs = SB_DIM // LANES
    n_tiles = seq // t
    j_idx = jnp.arange(2 * t)[:, None] % t
    s_idx = jnp.arange(2 * t)[None, :]
    tri = ((s_idx >= t) | (j_idx > s_idx)).astype(BF16)
    reps = LANES // SB_HEAD_DIM
    qg = jnp.tile(q_norm_g, reps).reshape(1, LANES)
    kg = jnp.tile(k_norm_g, reps).reshape(1, LANES)
    return pl.pallas_call(
        _sb_attn_kernel,
        grid=(bsz, n_pairs),
        in_specs=[pl.BlockSpec((seq, LANES), lambda b, p: (b, p)),
                  pl.BlockSpec((seq, LANES), lambda b, p: (b, n_pairs + p)),
                  pl.BlockSpec((seq, LANES), lambda b, p: (b, 2 * n_pairs + p)),
                  _const_spec((1, LANES)),
                  _const_spec((1, LANES)),
                  _const_spec((2 * t, 2 * t))],
        out_specs=pl.BlockSpec((seq, LANES), lambda b, p: (b, p)),
        out_shape=jax.ShapeDtypeStruct((rows, SB_DIM), F32),
        scratch_shapes=[pltpu.VMEM((n_tiles, LANES, t), BF16),
                        pltpu.VMEM((n_tiles, t, LANES), BF16),
                        pltpu.VMEM((2 * Q_TILE, LANES), BF16),
                        pltpu.VMEM((2 * Q_TILE, LANES), F32),
                        pltpu.VMEM((2 * Q_TILE, t), F32)],
        compiler_params=pltpu.CompilerParams(dimension_semantics=("parallel", "parallel"),
                                             vmem_limit_bytes=VMEM_LIMIT),
        name="sb_attn",
    )(qkv, qkv, qkv, qg, kg, tri)


def _out_proj_kernel(mix_ref, ya_ref, x_ref, mod_ref, ag_ref, w_ref, g2_ref, x1_ref, h2_ref):
    yb = _rms(ya_ref[...], ag_ref[...]).astype(BF16)
    o = (jnp.dot(mix_ref[...], w_ref[:CONV_DIM, :], preferred_element_type=F32)
         + jnp.dot(yb, w_ref[CONV_DIM:, :], preferred_element_type=F32))
    x1 = x_ref[...] + mod_ref[2:3, :] * o
    x1_ref[...] = x1
    h2 = _rms(x1, g2_ref[...]) * (1.0 + mod_ref[4:5, :]) + mod_ref[3:4, :]
    h2_ref[...] = h2.astype(BF16)


def _out_proj(mix_a, y_attn, x2, mod3, attn_out_g, w_out_b, norm2_g, seq):
    rows, d = x2.shape
    tm = ROW_TILE
    tiles_per_seq = seq // tm
    return pl.pallas_call(
        _out_proj_kernel,
        grid=(rows // tm,),
        in_specs=[pl.BlockSpec((tm, CONV_DIM), lambda i: (i, 0)),
                  pl.BlockSpec((tm, SB_DIM), lambda i: (i, 0)),
                  pl.BlockSpec((tm, d), lambda i: (i, 0)),
                  pl.BlockSpec((None, N_MOD, d), lambda i: (i // tiles_per_seq, 0, 0)),
                  _const_spec((1, SB_DIM)),
                  _const_spec(w_out_b.shape),
                  _const_spec((1, d))],
        out_specs=[pl.BlockSpec((tm, d), lambda i: (i, 0)),
                   pl.BlockSpec((tm, d), lambda i: (i, 0))],
        out_shape=[jax.ShapeDtypeStruct((rows, d), F32),
                   jax.ShapeDtypeStruct((rows, d), BF16)],
        compiler_params=pltpu.CompilerParams(dimension_semantics=("parallel",),
                                             vmem_limit_bytes=VMEM_LIMIT),
        name="out_proj",
    )(mix_a, y_attn, x2, mod3, attn_out_g.reshape(1, SB_DIM), w_out_b, norm2_g.reshape(1, d))


def _ffn_kernel(n_chunks, h2_ref, x1_ref, mod_ref, w1_ref, w2_ref, o_ref, acc_ref):
    h2 = h2_ref[...]
    cw = w1_ref.shape[1] // n_chunks
    for c in range(n_chunks):
        u = jnp.dot(h2, w1_ref[:, c * cw:(c + 1) * cw], preferred_element_type=F32)
        r = jnp.maximum(u, 0.0)
        f = jnp.dot((r * r).astype(BF16), w2_ref[c * cw:(c + 1) * cw, :],
                    preferred_element_type=F32)
        if c == 0:
            acc_ref[...] = f
        else:
            acc_ref[...] += f
    o_ref[...] = x1_ref[...] + mod_ref[5:6, :] * acc_ref[...]


def _ffn(h2, x1, mod3, w1_b, w2_b, seq):
    rows, d = x1.shape
    tm = ROW_TILE
    tiles_per_seq = seq // tm
    d_ff = w1_b.shape[1]
    n_chunks = 4
    return pl.pallas_call(
        functools.partial(_ffn_kernel, n_chunks),
        grid=(rows // tm,),
        in_specs=[pl.BlockSpec((tm, d), lambda i: (i, 0)),
                  pl.BlockSpec((tm, d), lambda i: (i, 0)),
                  pl.BlockSpec((None, N_MOD, d), lambda i: (i // tiles_per_seq, 0, 0)),
                  _const_spec((d, d_ff)),
                  _const_spec((d_ff, d))],
        out_specs=pl.BlockSpec((tm, d), lambda i: (i, 0)),
        out_shape=jax.ShapeDtypeStruct((rows, d), F32),
        scratch_shapes=[pltpu.VMEM((tm, d), F32)],
        compiler_params=pltpu.CompilerParams(dimension_semantics=("parallel",),
                                             vmem_limit_bytes=VMEM_LIMIT),
        name="ffn",
    )(h2, x1, mod3, w1_b, w2_b)


def kernel(x, c, w_ada, b_ada, norm1_g, w_in, conv_w, q_norm_g, k_norm_g, conv_out_g,
           attn_out_g, w_out, norm2_g, w_ff1, w_ff2):
    bsz, seq, d = x.shape
    assert seq % ROW_TILE == 0 and seq % Q_TILE == 0 and Q_TILE % ATT_TILE == 0
    assert w_in.shape[1] == 3 * CONV_DIM + 3 * SB_DIM
    x2 = x.reshape(bsz * seq, d)
    mod3 = _adaln(c, w_ada, b_ada).reshape(bsz, N_MOD, d)
    mix_a, qkv = _in_proj(x2, mod3, norm1_g, w_in.astype(BF16), conv_w, conv_out_g, seq)
    y_attn = _sb_attn(qkv, q_norm_g, k_norm_g, bsz, seq)
    x1, h2 = _out_proj(mix_a, y_attn, x2, mod3, attn_out_g, w_out.astype(BF16), norm2_g, seq)
    out = _ffn(h2, x1, mod3, w_ff1.astype(BF16), w_ff2.astype(BF16), seq)
    return out.reshape(bsz, seq, d)
```

```python
import functools

import jax
import jax.numpy as jnp
from jax import lax
from jax.experimental import pallas as pl
from jax.experimental.pallas import tpu as pltpu

EPS = 1e-6
CONV_DIM = 512
SB_DIM = 512
SB_HEAD_DIM = 64
N_MOD = 6
LANES = 128
ATT_TILE = 128
Q_TILE = 512
LOG2E = 1.4426950408889634
ROW_TILE = 512
VMEM_LIMIT = 56 * 1024 * 1024

F32 = jnp.float32
BF16 = jnp.bfloat16


def _split_bf16(a):
    hi = a.astype(BF16)
    lo = (a - hi.astype(F32)).astype(BF16)
    return hi, lo


def _const_spec(shape):
    return pl.BlockSpec(shape, lambda *_: (0,) * len(shape), pipeline_mode=pl.Buffered(1))


def _adaln_kernel(c_ref, w_ref, b_ref, o_ref):
    c = c_ref[...]
    s = c * (1.0 / (1.0 + jnp.exp(-c)))
    s_hi, s_lo = _split_bf16(s)
    w_hi, w_lo = _split_bf16(w_ref[...])
    dot = functools.partial(jnp.dot, preferred_element_type=F32)
    o_ref[...] = dot(s_hi, w_hi) + (dot(s_hi, w_lo) + dot(s_lo, w_hi)) + b_ref[...]


def _adaln(c, w_ada, b_ada):
    bsz, d = c.shape
    n = w_ada.shape[1]
    tn = 1024
    return pl.pallas_call(
        _adaln_kernel,
        grid=(n // tn,),
        in_specs=[pl.BlockSpec((bsz, d), lambda j: (0, 0)),
                  pl.BlockSpec((d, tn), lambda j: (0, j)),
                  pl.BlockSpec((1, tn), lambda j: (0, j))],
        out_specs=pl.BlockSpec((bsz, tn), lambda j: (0, j)),
        out_shape=jax.ShapeDtypeStruct((bsz, n), F32),
        compiler_params=pltpu.CompilerParams(dimension_semantics=("parallel",),
                                             vmem_limit_bytes=VMEM_LIMIT),
        name="adaln",
    )(c, w_ada, b_ada.reshape(1, n))


def _rms(xf, g):
    ms = jnp.mean(xf * xf, axis=-1, keepdims=True)
    return xf * lax.rsqrt(ms + EPS) * g


def _in_proj_kernel(tiles_per_seq, x_ref, mod_ref, g1_ref, w_ref, cw_ref, cg_ref,
                    mix_ref, qkv_ref, ext_ref):
    tm = x_ref.shape[0]
    i = pl.program_id(0)
    shift = mod_ref[0:1, :]
    scale = mod_ref[1:2, :]
    h = _rms(x_ref[...], g1_ref[...]) * (1.0 + scale) + shift
    hb = h.astype(BF16)

    n_conv = 3 * CONV_DIM
    p = jnp.dot(hb, w_ref[:, :n_conv], preferred_element_type=F32)
    b_gate = p[:, :CONV_DIM]
    cu = p[:, CONV_DIM:2 * CONV_DIM] * p[:, 2 * CONV_DIM:]

    @pl.when(i % tiles_per_seq == 0)
    def _():
        ext_ref[0:8, :] = jnp.zeros((8, CONV_DIM), F32)

    @pl.when(i % tiles_per_seq != 0)
    def _():
        ext_ref[0:8, :] = ext_ref[tm:tm + 8, :]

    ext_ref[8:tm + 8, :] = cu
    conv = (cw_ref[2:3, :] * cu + cw_ref[1:2, :] * ext_ref[7:tm + 7, :]
            + cw_ref[0:1, :] * ext_ref[6:tm + 6, :])
    mix_ref[...] = _rms(b_gate * conv, cg_ref[...]).astype(BF16)

    qkv_ref[...] = jnp.dot(hb, w_ref[:, n_conv:], preferred_element_type=F32)


def _in_proj(x2, mod3, norm1_g, w_in_b, conv_w, conv_out_g, seq):
    rows, d = x2.shape
    tm = ROW_TILE
    tiles_per_seq = seq // tm
    n_in = w_in_b.shape[1]
    n_qkv = n_in - 3 * CONV_DIM
    return pl.pallas_call(
        functools.partial(_in_proj_kernel, tiles_per_seq),
        grid=(rows // tm,),
        in_specs=[pl.BlockSpec((tm, d), lambda i: (i, 0)),
                  pl.BlockSpec((None, N_MOD, d), lambda i: (i // tiles_per_seq, 0, 0)),
                  _const_spec((1, d)),
                  _const_spec((d, n_in)),
                  _const_spec(conv_w.shape),
                  _const_spec((1, CONV_DIM))],
        out_specs=[pl.BlockSpec((tm, CONV_DIM), lambda i: (i, 0)),
                   pl.BlockSpec((tm, n_qkv), lambda i: (i, 0))],
        out_shape=[jax.ShapeDtypeStruct((rows, CONV_DIM), BF16),
                   jax.ShapeDtypeStruct((rows, n_qkv), F32)],
        scratch_shapes=[pltpu.VMEM((tm + 8, CONV_DIM), F32)],
        compiler_params=pltpu.CompilerParams(dimension_semantics=("arbitrary",),
                                             vmem_limit_bytes=VMEM_LIMIT),
        name="in_proj",
    )(x2, mod3, norm1_g.reshape(1, d), w_in_b, conv_w, conv_out_g.reshape(1, CONV_DIM))


def _head_rms(a, g, lo_lane):
    s = a * a
    sum_lo = jnp.sum(jnp.where(lo_lane, s, 0.0), axis=-1, keepdims=True)
    sum_hi = jnp.sum(jnp.where(lo_lane, 0.0, s), axis=-1, keepdims=True)
    ms = jnp.where(lo_lane, sum_lo, sum_hi) * (1.0 / SB_HEAD_DIM)
    return a * lax.rsqrt(ms + EPS) * g


def _sb_attn_kernel(q_ref, k_ref, v_ref, qg_ref, kg_ref, tri_ref, o_ref,
                    kbd_ref, vbd_ref, qn_ref, acc_ref, carry_ref,
                    z_ref, lb_ref, lst_ref, pre_ref, tot_ref):
    t = ATT_TILE
    sub = Q_TILE // t
    n_ktiles = k_ref.shape[0] // t
    n_qtiles = q_ref.shape[0] // Q_TILE
    lo_lane = lax.broadcasted_iota(jnp.int32, (t, LANES), 1) < SB_HEAD_DIM
    lo_row = lax.broadcasted_iota(jnp.int32, (LANES, t), 0) < SB_HEAD_DIM
    row = lax.broadcasted_iota(jnp.int32, (t, 2 * t), 0)
    col = lax.broadcasted_iota(jnp.int32, (t, 2 * t), 1) & (t - 1)
    causal = col < row

    def prep(j, _):
        r0 = pl.multiple_of(j * t, t)
        kt = _head_rms(k_ref[pl.ds(r0, t), :], kg_ref[...], lo_lane).T
        kbd_ref[j, :, :t] = jnp.where(lo_row, kt, 0.0).astype(BF16)
        kbd_ref[j, :, t:] = jnp.where(lo_row, 0.0, kt).astype(BF16)
        v = v_ref[pl.ds(r0, t), :]
        vbd_ref[j, :t, :] = jnp.where(lo_lane, v, 0.0).astype(BF16)
        vbd_ref[j, t:, :] = jnp.where(lo_lane, 0.0, v).astype(BF16)
        return 0

    lax.fori_loop(0, n_ktiles, prep, 0)


    def qk(step):
        j, d, _ = step
        z_ref[d * t:, :] = jnp.dot(qn_ref[d * t:, :], kbd_ref[j], preferred_element_type=F32)

    def scores(step):
        _, d, diag = step
        r0 = d * t
        n = Q_TILE - r0
        z = z_ref[r0:, :]
        sp = jnp.log2(1.0 + jnp.exp2(-jnp.abs(z)))
        log_beta = jnp.minimum(z, 0.0) - sp
        log_1mb = log_beta - z
        if diag:
            masked = jnp.where(causal, log_1mb[:t], 0.0)
            log_1mb = masked if n == t else jnp.concatenate([masked, log_1mb[t:]], axis=0)
        lb_ref[r0:, :] = log_beta
        l_hi, l_lo = _split_bf16(log_1mb)
        lst_ref[r0:Q_TILE, :t] = l_hi[:, :t]
        lst_ref[r0:Q_TILE, t:] = l_lo[:, :t]
        lst_ref[Q_TILE + r0:, :t] = l_hi[:, t:]
        lst_ref[Q_TILE + r0:, t:] = l_lo[:, t:]

    def cumsum(step):
        r0 = step[1] * t
        tri = tri_ref[...]
        cum_a = jnp.dot(lst_ref[r0:Q_TILE, :], tri, preferred_element_type=F32)
        cum_b = jnp.dot(lst_ref[Q_TILE + r0:, :], tri, preferred_element_type=F32)
        pre_ref[r0:, :t] = lb_ref[r0:, :t] + cum_a[:, :t]
        pre_ref[r0:, t:] = lb_ref[r0:, t:] + cum_b[:, :t]
        tot_ref[r0:, :t] = cum_a[:, t:]
        tot_ref[r0:, t:] = cum_b[:, t:]

    def weights(step):
        j, d, diag = step
        r0 = d * t
        n = Q_TILE - r0
        if diag:
            a = jnp.where(causal, jnp.exp2(pre_ref[r0:r0 + t, :]), 0.0)
            carry_ref[r0:r0 + t, :] = tot_ref[r0:r0 + t, :]
            if n > t:
                carry = carry_ref[r0 + t:, :]
                a = jnp.concatenate([a, jnp.exp2(pre_ref[r0 + t:, :] + carry)], axis=0)
                carry_ref[r0 + t:, :] = carry + tot_ref[r0 + t:, :]
        else:
            carry = carry_ref[...]
            a = jnp.exp2(pre_ref[...] + carry)
            carry_ref[...] = carry + tot_ref[...]
        av = jnp.dot(a.astype(BF16), vbd_ref[j], preferred_element_type=F32)
        if diag:
            acc_ref[r0:r0 + t, :] = av[:t]
            if n > t:
                acc_ref[r0 + t:, :] += av[t:]
        else:
            acc_ref[...] += av

    def q_tile(qi, _):
        q0 = pl.multiple_of(qi * Q_TILE, Q_TILE)
        qn = _head_rms(q_ref[pl.ds(q0, Q_TILE), :], qg_ref[...],
                       lax.broadcasted_iota(jnp.int32, (Q_TILE, LANES), 1) < SB_HEAD_DIM)
        qn_ref[...] = (qn * (SB_HEAD_DIM ** -0.5 * LOG2E)).astype(BF16)
        jd = qi * sub
        diag_steps = [(jd + d, d, True) for d in reversed(range(sub))]

        def full(j):
            return (j, 0, False)

        def group(steps_by_stage):
            for s in reversed(range(n_stages)):
                if steps_by_stage[s] is not None:
                    stages[s](steps_by_stage[s])

        def step_at(k):
            return diag_steps[k] if k < sub else full(jd + sub - 1 - k)

        for g in range(sub):
            group([diag_steps[g - s] if g - s >= 0 else None for s in range(n_stages)])

        @pl.when(qi == 0)
        def _():
            for g in range(sub, sub + n_stages - 1):
                group([diag_steps[g - s] if 0 <= g - s < sub else None
                       for s in range(n_stages)])

        @pl.when(qi > 0)
        def _():
            for g in range(sub, sub + n_stages - 1):
                group([step_at(g - s) for s in range(n_stages)])

            def steady(j):
                group([full(j - (n_stages - 1 - s)) for s in range(n_stages)])

            steady(jd - 1)

            def body(it, _):
                steady(jd - 2 - 2 * it)
                steady(jd - 3 - 2 * it)
                return 0

            lax.fori_loop(0, (jd - n_stages) // 2, body, 0)
            for g in range(1, n_stages):
                group([full(s - g) if s - g >= 0 else None for s in range(n_stages)])

        o_ref[pl.ds(q0, Q_TILE), :] = acc_ref[...]
        return 0

    stages = (qk, scores, cumsum, weights)
    n_stages = len(stages)

    lax.fori_loop(0, n_qtiles, q_tile, 0)


def _sb_attn(qkv, q_norm_g, k_norm_g, bsz, seq):
    rows = qkv.shape[0]
    t = ATT_TILE
    n_pairs = SB_DIM // LANES
    n_tiles = seq // t
    j_idx = jnp.arange(2 * t)[:, None] % t
    s_idx = jnp.arange(2 * t)[None, :]
    tri = ((s_idx >= t) | (j_idx > s_idx)).astype(BF16)
    reps = LANES // SB_HEAD_DIM
    qg = jnp.tile(q_norm_g, reps).reshape(1, LANES)
    kg = jnp.tile(k_norm_g, reps).reshape(1, LANES)
    return pl.pallas_call(
        _sb_attn_kernel,
        grid=(bsz, n_pairs),
        in_specs=[pl.BlockSpec((seq, LANES), lambda b, p: (b, p)),
                  pl.BlockSpec((seq, LANES), lambda b, p: (b, n_pairs + p)),
                  pl.BlockSpec((seq, LANES), lambda b, p: (b, 2 * n_pairs + p)),
                  _const_spec((1, LANES)),
                  _const_spec((1, LANES)),
                  _const_spec((2 * t, 2 * t))],
        out_specs=pl.BlockSpec((seq, LANES), lambda b, p: (b, p)),
        out_shape=jax.ShapeDtypeStruct((rows, SB_DIM), F32),
        scratch_shapes=[pltpu.VMEM((n_tiles, LANES, 2 * t), BF16),
                        pltpu.VMEM((n_tiles, 2 * t, LANES), BF16),
                        pltpu.VMEM((Q_TILE, LANES), BF16),
                        pltpu.VMEM((Q_TILE, LANES), F32),
                        pltpu.VMEM((Q_TILE, 2 * t), F32),
                        pltpu.VMEM((Q_TILE, 2 * t), F32),
                        pltpu.VMEM((Q_TILE, 2 * t), F32),
                        pltpu.VMEM((2 * Q_TILE, 2 * t), BF16),
                        pltpu.VMEM((Q_TILE, 2 * t), F32),
                        pltpu.VMEM((Q_TILE, 2 * t), F32)],
        compiler_params=pltpu.CompilerParams(dimension_semantics=("parallel", "parallel"),
                                             vmem_limit_bytes=VMEM_LIMIT),
        name="sb_attn",
    )(qkv, qkv, qkv, qg, kg, tri)


def _out_proj_kernel(mix_ref, ya_ref, x_ref, mod_ref, ag_ref, w_ref, g2_ref, x1_ref, h2_ref):
    yb = _rms(ya_ref[...], ag_ref[...]).astype(BF16)
    o = (jnp.dot(mix_ref[...], w_ref[:CONV_DIM, :], preferred_element_type=F32)
         + jnp.dot(yb, w_ref[CONV_DIM:, :], preferred_element_type=F32))
    x1 = x_ref[...] + mod_ref[2:3, :] * o
    x1_ref[...] = x1
    h2 = _rms(x1, g2_ref[...]) * (1.0 + mod_ref[4:5, :]) + mod_ref[3:4, :]
    h2_ref[...] = h2.astype(BF16)


def _out_proj(mix_a, y_attn, x2, mod3, attn_out_g, w_out_b, norm2_g, seq):
    rows, d = x2.shape
    tm = ROW_TILE
    tiles_per_seq = seq // tm
    return pl.pallas_call(
        _out_proj_kernel,
        grid=(rows // tm,),
        in_specs=[pl.BlockSpec((tm, CONV_DIM), lambda i: (i, 0)),
                  pl.BlockSpec((tm, SB_DIM), lambda i: (i, 0)),
                  pl.BlockSpec((tm, d), lambda i: (i, 0)),
                  pl.BlockSpec((None, N_MOD, d), lambda i: (i // tiles_per_seq, 0, 0)),
                  _const_spec((1, SB_DIM)),
                  _const_spec(w_out_b.shape),
                  _const_spec((1, d))],
        out_specs=[pl.BlockSpec((tm, d), lambda i: (i, 0)),
                   pl.BlockSpec((tm, d), lambda i: (i, 0))],
        out_shape=[jax.ShapeDtypeStruct((rows, d), F32),
                   jax.ShapeDtypeStruct((rows, d), BF16)],
        compiler_params=pltpu.CompilerParams(dimension_semantics=("parallel",),
                                             vmem_limit_bytes=VMEM_LIMIT),
        name="out_proj",
    )(mix_a, y_attn, x2, mod3, attn_out_g.reshape(1, SB_DIM), w_out_b, norm2_g.reshape(1, d))


def _ffn_kernel(n_chunks, h2_ref, x1_ref, mod_ref, w1_ref, w2_ref, o_ref, acc_ref):
    h2 = h2_ref[...]
    cw = w1_ref.shape[1] // n_chunks
    for c in range(n_chunks):
        u = jnp.dot(h2, w1_ref[:, c * cw:(c + 1) * cw], preferred_element_type=F32)
        r = jnp.maximum(u, 0.0)
        f = jnp.dot((r * r).astype(BF16), w2_ref[c * cw:(c + 1) * cw, :],
                    preferred_element_type=F32)
        if c == 0:
            acc_ref[...] = f
        else:
            acc_ref[...] += f
    o_ref[...] = x1_ref[...] + mod_ref[5:6, :] * acc_ref[...]


def _ffn(h2, x1, mod3, w1_b, w2_b, seq):
    rows, d = x1.shape
    tm = ROW_TILE
    tiles_per_seq = seq // tm
    d_ff = w1_b.shape[1]
    n_chunks = 4
    return pl.pallas_call(
        functools.partial(_ffn_kernel, n_chunks),
        grid=(rows // tm,),
        in_specs=[pl.BlockSpec((tm, d), lambda i: (i, 0)),
                  pl.BlockSpec((tm, d), lambda i: (i, 0)),
                  pl.BlockSpec((None, N_MOD, d), lambda i: (i // tiles_per_seq, 0, 0)),
                  _const_spec((d, d_ff)),
                  _const_spec((d_ff, d))],
        out_specs=pl.BlockSpec((tm, d), lambda i: (i, 0)),
        out_shape=jax.ShapeDtypeStruct((rows, d), F32),
        scratch_shapes=[pltpu.VMEM((tm, d), F32)],
        compiler_params=pltpu.CompilerParams(dimension_semantics=("parallel",),
                                             vmem_limit_bytes=VMEM_LIMIT),
        name="ffn",
    )(h2, x1, mod3, w1_b, w2_b)


def kernel(x, c, w_ada, b_ada, norm1_g, w_in, conv_w, q_norm_g, k_norm_g, conv_out_g,
           attn_out_g, w_out, norm2_g, w_ff1, w_ff2):
    bsz, seq, d = x.shape
    assert seq % ROW_TILE == 0 and seq % Q_TILE == 0 and Q_TILE % ATT_TILE == 0
    assert w_in.shape[1] == 3 * CONV_DIM + 3 * SB_DIM
    x2 = x.reshape(bsz * seq, d)
    mod3 = _adaln(c, w_ada, b_ada).reshape(bsz, N_MOD, d)
    mix_a, qkv = _in_proj(x2, mod3, norm1_g, w_in.astype(BF16), conv_w, conv_out_g, seq)
    y_attn = _sb_attn(qkv, q_norm_g, k_norm_g, bsz, seq)
    x1, h2 = _out_proj(mix_a, y_attn, x2, mod3, attn_out_g, w_out.astype(BF16), norm2_g, seq)
    out = _ffn(h2, x1, mod3, w_ff1.astype(BF16), w_ff2.astype(BF16), seq)
    return out.reshape(bsz, seq, d)
```

```python
import functools

import jax
import jax.numpy as jnp
from jax import lax
from jax.experimental import pallas as pl
from jax.experimental.pallas import tpu as pltpu

EPS = 1e-6
CONV_DIM = 512
SB_DIM = 512
SB_HEAD_DIM = 64
N_MOD = 6
LANES = 128
ATT_TILE = 128
Q_TILE = 512
PREP_TILES = 4
LOG2E = 1.4426950408889634
ROW_TILE = 512
VMEM_LIMIT = 56 * 1024 * 1024

F32 = jnp.float32
BF16 = jnp.bfloat16


def _split_bf16(a):
    hi = a.astype(BF16)
    lo = (a - hi.astype(F32)).astype(BF16)
    return hi, lo


def _const_spec(shape):
    return pl.BlockSpec(shape, lambda *_: (0,) * len(shape), pipeline_mode=pl.Buffered(1))


def _adaln_kernel(c_ref, w_ref, b_ref, o_ref):
    c = c_ref[...]
    s = c * (1.0 / (1.0 + jnp.exp(-c)))
    s_hi, s_lo = _split_bf16(s)
    w_hi, w_lo = _split_bf16(w_ref[...])
    dot = functools.partial(jnp.dot, preferred_element_type=F32)
    o_ref[...] = dot(s_hi, w_hi) + (dot(s_hi, w_lo) + dot(s_lo, w_hi)) + b_ref[...]


def _adaln(c, w_ada, b_ada):
    bsz, d = c.shape
    n = w_ada.shape[1]
    tn = 1024
    return pl.pallas_call(
        _adaln_kernel,
        grid=(n // tn,),
        in_specs=[pl.BlockSpec((bsz, d), lambda j: (0, 0)),
                  pl.BlockSpec((d, tn), lambda j: (0, j)),
                  pl.BlockSpec((1, tn), lambda j: (0, j))],
        out_specs=pl.BlockSpec((bsz, tn), lambda j: (0, j)),
        out_shape=jax.ShapeDtypeStruct((bsz, n), F32),
        compiler_params=pltpu.CompilerParams(dimension_semantics=("parallel",),
                                             vmem_limit_bytes=VMEM_LIMIT),
        name="adaln",
    )(c, w_ada, b_ada.reshape(1, n))


def _rms(xf, g):
    ms = jnp.mean(xf * xf, axis=-1, keepdims=True)
    return xf * lax.rsqrt(ms + EPS) * g


def _in_proj_kernel(tiles_per_seq, x_ref, mod_ref, g1_ref, w_ref, cw_ref, cg_ref,
                    mix_ref, qkv_ref, ext_ref):
    tm = x_ref.shape[0]
    i = pl.program_id(0)
    shift = mod_ref[0:1, :]
    scale = mod_ref[1:2, :]
    h = _rms(x_ref[...], g1_ref[...]) * (1.0 + scale) + shift
    hb = h.astype(BF16)

    n_conv = 3 * CONV_DIM
    p = jnp.dot(hb, w_ref[:, :n_conv], preferred_element_type=F32)
    b_gate = p[:, :CONV_DIM]
    cu = p[:, CONV_DIM:2 * CONV_DIM] * p[:, 2 * CONV_DIM:]

    @pl.when(i % tiles_per_seq == 0)
    def _():
        ext_ref[0:8, :] = jnp.zeros((8, CONV_DIM), F32)

    @pl.when(i % tiles_per_seq != 0)
    def _():
        ext_ref[0:8, :] = ext_ref[tm:tm + 8, :]

    ext_ref[8:tm + 8, :] = cu
    conv = (cw_ref[2:3, :] * cu + cw_ref[1:2, :] * ext_ref[7:tm + 7, :]
            + cw_ref[0:1, :] * ext_ref[6:tm + 6, :])
    mix_ref[...] = _rms(b_gate * conv, cg_ref[...]).astype(BF16)

    qkv_ref[...] = jnp.dot(hb, w_ref[:, n_conv:], preferred_element_type=F32)


def _in_proj(x2, mod3, norm1_g, w_in_b, conv_w, conv_out_g, seq):
    rows, d = x2.shape
    tm = ROW_TILE
    tiles_per_seq = seq // tm
    n_in = w_in_b.shape[1]
    n_qkv = n_in - 3 * CONV_DIM
    return pl.pallas_call(
        functools.partial(_in_proj_kernel, tiles_per_seq),
        grid=(rows // tm,),
        in_specs=[pl.BlockSpec((tm, d), lambda i: (i, 0)),
                  pl.BlockSpec((None, N_MOD, d), lambda i: (i // tiles_per_seq, 0, 0)),
                  _const_spec((1, d)),
                  _const_spec((d, n_in)),
                  _const_spec(conv_w.shape),
                  _const_spec((1, CONV_DIM))],
        out_specs=[pl.BlockSpec((tm, CONV_DIM), lambda i: (i, 0)),
                   pl.BlockSpec((tm, n_qkv), lambda i: (i, 0))],
        out_shape=[jax.ShapeDtypeStruct((rows, CONV_DIM), BF16),
                   jax.ShapeDtypeStruct((rows, n_qkv), F32)],
        scratch_shapes=[pltpu.VMEM((tm + 8, CONV_DIM), F32)],
        compiler_params=pltpu.CompilerParams(dimension_semantics=("arbitrary",),
                                             vmem_limit_bytes=VMEM_LIMIT),
        name="in_proj",
    )(x2, mod3, norm1_g.reshape(1, d), w_in_b, conv_w, conv_out_g.reshape(1, CONV_DIM))


def _head_rms(a, g, hsum):
    s_hi, s_lo = _split_bf16(a * a)
    ssq = jnp.dot(jnp.concatenate([s_hi, s_lo], axis=1), hsum, preferred_element_type=F32)
    return a * lax.rsqrt(ssq * (1.0 / SB_HEAD_DIM) + EPS) * g


def _sb_attn_kernel(q_ref, k_ref, v_ref, qg_ref, kg_ref, tri_ref, hsum_ref, o_ref,
                    kbd_ref, vbd_ref, qn_ref, acc_ref, carry_ref,
                    z_ref, lb_ref, lst_ref, pre_ref, tot_ref):
    t = ATT_TILE
    sub = Q_TILE // t
    n_ktiles = k_ref.shape[0] // t
    n_qtiles = q_ref.shape[0] // Q_TILE
    lo_lane = lax.broadcasted_iota(jnp.int32, (t, LANES), 1) < SB_HEAD_DIM
    lo_row = lax.broadcasted_iota(jnp.int32, (LANES, t), 0) < SB_HEAD_DIM
    row = lax.broadcasted_iota(jnp.int32, (t, 2 * t), 0)
    col = lax.broadcasted_iota(jnp.int32, (t, 2 * t), 1) & (t - 1)
    causal = col < row

    def prep(c, _):
        for i in range(PREP_TILES):
            j = c * PREP_TILES + i
            r0 = pl.multiple_of(j * t, t)
            kt = _head_rms(k_ref[pl.ds(r0, t), :], kg_ref[...], hsum_ref[...]).T
            kbd_ref[j, :, :t] = jnp.where(lo_row, kt, 0.0).astype(BF16)
            kbd_ref[j, :, t:] = jnp.where(lo_row, 0.0, kt).astype(BF16)
            v = v_ref[pl.ds(r0, t), :]
            vbd_ref[j, :t, :] = jnp.where(lo_lane, v, 0.0).astype(BF16)
            vbd_ref[j, t:, :] = jnp.where(lo_lane, 0.0, v).astype(BF16)
            qn = _head_rms(q_ref[pl.ds(r0, t), :], qg_ref[...], hsum_ref[...])
            qn_ref[pl.ds(r0, t), :] = (qn * (SB_HEAD_DIM ** -0.5 * LOG2E)).astype(BF16)
        return 0

    lax.fori_loop(0, n_ktiles // PREP_TILES, prep, 0)


    def qk(step):
        j, d, _, q0 = step
        q = qn_ref[pl.ds(q0 + d * t, Q_TILE - d * t), :]
        z_ref[d * t:, :] = jnp.dot(q, kbd_ref[j], preferred_element_type=F32)

    def scores(step):
        _, d, diag, _ = step
        r0 = d * t
        n = Q_TILE - r0
        z = z_ref[r0:, :]
        sp = jnp.log2(1.0 + jnp.exp2(-jnp.abs(z)))
        log_beta = jnp.minimum(z, 0.0) - sp
        log_1mb = log_beta - z
        if diag:
            masked = jnp.where(causal, log_1mb[:t], 0.0)
            log_1mb = masked if n == t else jnp.concatenate([masked, log_1mb[t:]], axis=0)
        lb_ref[r0:, :] = log_beta
        l16 = log_1mb.astype(BF16)
        lst_ref[0, r0:, :] = l16[:, :t]
        lst_ref[1, r0:, :] = l16[:, t:]

    def cumsum(step):
        r0 = step[1] * t
        for h in range(2):
            c = h * t
            cum = jnp.dot(lst_ref[h, r0:, :], tri_ref[...], preferred_element_type=F32)
            pre_ref[r0:, c:c + t] = lb_ref[r0:, c:c + t] + cum[:, :t]
            tot_ref[r0:, c:c + t] = cum[:, t:]

    def weights(step):
        j, d, diag, _ = step
        r0 = d * t
        n = Q_TILE - r0
        if diag:
            a = jnp.where(causal, jnp.exp2(pre_ref[r0:r0 + t, :]), 0.0)
            carry_ref[r0:r0 + t, :] = tot_ref[r0:r0 + t, :]
            if n > t:
                carry = carry_ref[r0 + t:, :]
                a = jnp.concatenate([a, jnp.exp2(pre_ref[r0 + t:, :] + carry)], axis=0)
                carry_ref[r0 + t:, :] = carry + tot_ref[r0 + t:, :]
        else:
            carry = carry_ref[...]
            a = jnp.exp2(pre_ref[...] + carry)
            carry_ref[...] = carry + tot_ref[...]
        av = jnp.dot(a.astype(BF16), vbd_ref[j], preferred_element_type=F32)
        if diag:
            acc_ref[r0:r0 + t, :] = av[:t]
            if n > t:
                acc_ref[r0 + t:, :] += av[t:]
        else:
            acc_ref[...] += av

    stages = (qk, scores, cumsum, weights)
    n_stages = len(stages)

    steps = []
    for qi in range(n_qtiles):
        jd = qi * sub
        tile_steps = [(jd + d, d, True, qi * Q_TILE) for d in reversed(range(sub))]
        tile_steps += [(j, 0, False, qi * Q_TILE) for j in reversed(range(jd))]
        steps += [(st, i == len(tile_steps) - 1) for i, st in enumerate(tile_steps)]
    for g in range(len(steps) + n_stages - 1):
        for s in reversed(range(n_stages)):
            if 0 <= g - s < len(steps):
                st, last_of_tile = steps[g - s]
                stages[s](st)
                if last_of_tile and s == n_stages - 1:
                    o_ref[st[3]:st[3] + Q_TILE, :] = acc_ref[...]


def _sb_attn(qkv, q_norm_g, k_norm_g, bsz, seq):
    rows = qkv.shape[0]
    t = ATT_TILE
    n_pairs = SB_DIM // LANES
    n_tiles = seq // t
    j_idx = jnp.arange(t)[:, None]
    s_idx = jnp.arange(2 * t)[None, :]
    tri = ((s_idx >= t) | (j_idx > s_idx)).astype(BF16)
    lane_head = jnp.arange(LANES) // SB_HEAD_DIM
    hsum = (jnp.tile(lane_head, 2)[:, None] == lane_head[None, :]).astype(BF16)
    reps = LANES // SB_HEAD_DIM
    qg = jnp.tile(q_norm_g, reps).reshape(1, LANES)
    kg = jnp.tile(k_norm_g, reps).reshape(1, LANES)
    return pl.pallas_call(
        _sb_attn_kernel,
        grid=(bsz, n_pairs),
        in_specs=[pl.BlockSpec((seq, LANES), lambda b, p: (b, p)),
                  pl.BlockSpec((seq, LANES), lambda b, p: (b, n_pairs + p)),
                  pl.BlockSpec((seq, LANES), lambda b, p: (b, 2 * n_pairs + p)),
                  _const_spec((1, LANES)),
                  _const_spec((1, LANES)),
                  _const_spec((t, 2 * t)),
                  _const_spec((2 * LANES, LANES))],
        out_specs=pl.BlockSpec((seq, LANES), lambda b, p: (b, p)),
        out_shape=jax.ShapeDtypeStruct((rows, SB_DIM), F32),
        scratch_shapes=[pltpu.VMEM((n_tiles, LANES, 2 * t), BF16),
                        pltpu.VMEM((n_tiles, 2 * t, LANES), BF16),
                        pltpu.VMEM((seq, LANES), BF16),
                        pltpu.VMEM((Q_TILE, LANES), F32),
                        pltpu.VMEM((Q_TILE, 2 * t), F32),
                        pltpu.VMEM((Q_TILE, 2 * t), F32),
                        pltpu.VMEM((Q_TILE, 2 * t), F32),
                        pltpu.VMEM((2, Q_TILE, t), BF16),
                        pltpu.VMEM((Q_TILE, 2 * t), F32),
                        pltpu.VMEM((Q_TILE, 2 * t), F32)],
        compiler_params=pltpu.CompilerParams(dimension_semantics=("parallel", "parallel"),
                                             vmem_limit_bytes=VMEM_LIMIT),
        name="sb_attn",
    )(qkv, qkv, qkv, qg, kg, tri, hsum)


def _out_proj_kernel(mix_ref, ya_ref, x_ref, mod_ref, ag_ref, w_ref, g2_ref, x1_ref, h2_ref):
    yb = _rms(ya_ref[...], ag_ref[...]).astype(BF16)
    o = (jnp.dot(mix_ref[...], w_ref[:CONV_DIM, :], preferred_element_type=F32)
         + jnp.dot(yb, w_ref[CONV_DIM:, :], preferred_element_type=F32))
    x1 = x_ref[...] + mod_ref[2:3, :] * o
    x1_ref[...] = x1
    h2 = _rms(x1, g2_ref[...]) * (1.0 + mod_ref[4:5, :]) + mod_ref[3:4, :]
    h2_ref[...] = h2.astype(BF16)


def _out_proj(mix_a, y_attn, x2, mod3, attn_out_g, w_out_b, norm2_g, seq):
    rows, d = x2.shape
    tm = ROW_TILE
    tiles_per_seq = seq // tm
    return pl.pallas_call(
        _out_proj_kernel,
        grid=(rows // tm,),
        in_specs=[pl.BlockSpec((tm, CONV_DIM), lambda i: (i, 0)),
                  pl.BlockSpec((tm, SB_DIM), lambda i: (i, 0)),
                  pl.BlockSpec((tm, d), lambda i: (i, 0)),
                  pl.BlockSpec((None, N_MOD, d), lambda i: (i // tiles_per_seq, 0, 0)),
                  _const_spec((1, SB_DIM)),
                  _const_spec(w_out_b.shape),
                  _const_spec((1, d))],
        out_specs=[pl.BlockSpec((tm, d), lambda i: (i, 0)),
                   pl.BlockSpec((tm, d), lambda i: (i, 0))],
        out_shape=[jax.ShapeDtypeStruct((rows, d), F32),
                   jax.ShapeDtypeStruct((rows, d), BF16)],
        compiler_params=pltpu.CompilerParams(dimension_semantics=("parallel",),
                                             vmem_limit_bytes=VMEM_LIMIT),
        name="out_proj",
    )(mix_a, y_attn, x2, mod3, attn_out_g.reshape(1, SB_DIM), w_out_b, norm2_g.reshape(1, d))


def _ffn_kernel(n_chunks, h2_ref, x1_ref, mod_ref, w1_ref, w2_ref, o_ref, acc_ref):
    h2 = h2_ref[...]
    cw = w1_ref.shape[1] // n_chunks
    for c in range(n_chunks):
        u = jnp.dot(h2, w1_ref[:, c * cw:(c + 1) * cw], preferred_element_type=F32)
        r = jnp.maximum(u, 0.0)
        f = jnp.dot((r * r).astype(BF16), w2_ref[c * cw:(c + 1) * cw, :],
                    preferred_element_type=F32)
        if c == 0:
            acc_ref[...] = f
        else:
            acc_ref[...] += f
    o_ref[...] = x1_ref[...] + mod_ref[5:6, :] * acc_ref[...]


def _ffn(h2, x1, mod3, w1_b, w2_b, seq):
    rows, d = x1.shape
    tm = ROW_TILE
    tiles_per_seq = seq // tm
    d_ff = w1_b.shape[1]
    n_chunks = 4
    return pl.pallas_call(
        functools.partial(_ffn_kernel, n_chunks),
        grid=(rows // tm,),
        in_specs=[pl.BlockSpec((tm, d), lambda i: (i, 0)),
                  pl.BlockSpec((tm, d), lambda i: (i, 0)),
                  pl.BlockSpec((None, N_MOD, d), lambda i: (i // tiles_per_seq, 0, 0)),
                  _const_spec((d, d_ff)),
                  _const_spec((d_ff, d))],
        out_specs=pl.BlockSpec((tm, d), lambda i: (i, 0)),
        out_shape=jax.ShapeDtypeStruct((rows, d), F32),
        scratch_shapes=[pltpu.VMEM((tm, d), F32)],
        compiler_params=pltpu.CompilerParams(dimension_semantics=("parallel",),
                                             vmem_limit_bytes=VMEM_LIMIT),
        name="ffn",
    )(h2, x1, mod3, w1_b, w2_b)


def kernel(x, c, w_ada, b_ada, norm1_g, w_in, conv_w, q_norm_g, k_norm_g, conv_out_g,
           attn_out_g, w_out, norm2_g, w_ff1, w_ff2):
    bsz, seq, d = x.shape
    assert seq % ROW_TILE == 0 and seq % Q_TILE == 0 and Q_TILE % ATT_TILE == 0
    assert w_in.shape[1] == 3 * CONV_DIM + 3 * SB_DIM
    x2 = x.reshape(bsz * seq, d)
    mod3 = _adaln(c, w_ada, b_ada).reshape(bsz, N_MOD, d)
    mix_a, qkv = _in_proj(x2, mod3, norm1_g, w_in.astype(BF16), conv_w, conv_out_g, seq)
    y_attn = _sb_attn(qkv, q_norm_g, k_norm_g, bsz, seq)
    x1, h2 = _out_proj(mix_a, y_attn, x2, mod3, attn_out_g, w_out.astype(BF16), norm2_g, seq)
    out = _ffn(h2, x1, mod3, w_ff1.astype(BF16), w_ff2.astype(BF16), seq)
    return out.reshape(bsz, seq, d)
```

```python
import functools

import jax
import jax.numpy as jnp
from jax import lax
from jax.experimental import pallas as pl
from jax.experimental.pallas import tpu as pltpu

EPS = 1e-6
CONV_DIM = 512
SB_DIM = 512
SB_HEAD_DIM = 64
N_MOD = 6
LANES = 128
ATT_TILE = 128
Q_TILE = 512
PREP_TILES = 4
LOG2E = 1.4426950408889634
ROW_TILE = 512
IN_ROW_TILE = 1024
IN_SUB_ROWS = 256
FFN_SUB_ROWS = 512
FFN_CHUNK = 512
VMEM_LIMIT = 56 * 1024 * 1024

F32 = jnp.float32
BF16 = jnp.bfloat16


def _split_bf16(a):
    hi = a.astype(BF16)
    lo = (a - hi.astype(F32)).astype(BF16)
    return hi, lo


def _const_spec(shape):
    return pl.BlockSpec(shape, lambda *_: (0,) * len(shape), pipeline_mode=pl.Buffered(1))


def _adaln_kernel(c_ref, w_ref, b_ref, o_ref):
    c = c_ref[...]
    s = c * (1.0 / (1.0 + jnp.exp(-c)))
    s_hi, s_lo = _split_bf16(s)
    w_hi, w_lo = _split_bf16(w_ref[...])
    dot = functools.partial(jnp.dot, preferred_element_type=F32)
    o_ref[...] = dot(s_hi, w_hi) + (dot(s_hi, w_lo) + dot(s_lo, w_hi)) + b_ref[...]


def _adaln(c, w_ada, b_ada):
    bsz, d = c.shape
    n = w_ada.shape[1]
    tn = 1024
    return pl.pallas_call(
        _adaln_kernel,
        grid=(n // tn,),
        in_specs=[pl.BlockSpec((bsz, d), lambda j: (0, 0)),
                  pl.BlockSpec((d, tn), lambda j: (0, j)),
                  pl.BlockSpec((1, tn), lambda j: (0, j))],
        out_specs=pl.BlockSpec((bsz, tn), lambda j: (0, j)),
        out_shape=jax.ShapeDtypeStruct((bsz, n), F32),
        compiler_params=pltpu.CompilerParams(dimension_semantics=("parallel",),
                                             vmem_limit_bytes=VMEM_LIMIT),
        name="adaln",
    )(c, w_ada, b_ada.reshape(1, n))


def _rms(xf, g):
    ms = jnp.mean(xf * xf, axis=-1, keepdims=True)
    return xf * lax.rsqrt(ms + EPS) * g


def _in_proj_kernel(tiles_per_seq, x_ref, mod_ref, g1_ref, w_ref, cw_ref, cg_ref,
                    mix_ref, qkv_ref, ext_ref, hb_ref, p_ref):
    tm = x_ref.shape[0]
    i = pl.program_id(0)

    @pl.when(i % tiles_per_seq == 0)
    def _():
        ext_ref[0:8, :] = jnp.zeros((8, CONV_DIM), F32)

    @pl.when(i % tiles_per_seq != 0)
    def _():
        ext_ref[0:8, :] = ext_ref[tm:tm + 8, :]

    shift = mod_ref[0:1, :]
    scale = mod_ref[1:2, :]
    n_conv = 3 * CONV_DIM
    n_sub = tm // IN_SUB_ROWS

    def rows(k):
        return slice(k * IN_SUB_ROWS, (k + 1) * IN_SUB_ROWS)

    def norm(k):
        h = _rms(x_ref[rows(k), :], g1_ref[...]) * (1.0 + scale) + shift
        hb_ref[rows(k), :] = h.astype(BF16)

    def project(k):
        hb = hb_ref[rows(k), :]
        p_ref[k % 2] = jnp.dot(hb, w_ref[:, :n_conv], preferred_element_type=F32)
        qkv_ref[rows(k), :] = jnp.dot(hb, w_ref[:, n_conv:], preferred_element_type=F32)

    def conv_mixer(k):
        r0 = k * IN_SUB_ROWS
        r1 = r0 + IN_SUB_ROWS
        b_gate = p_ref[k % 2, :, :CONV_DIM]
        cu = p_ref[k % 2, :, CONV_DIM:2 * CONV_DIM] * p_ref[k % 2, :, 2 * CONV_DIM:]
        ext_ref[r0 + 8:r1 + 8, :] = cu
        conv = (cw_ref[2:3, :] * cu + cw_ref[1:2, :] * ext_ref[r0 + 7:r1 + 7, :]
                + cw_ref[0:1, :] * ext_ref[r0 + 6:r1 + 6, :])
        mix_ref[rows(k), :] = _rms(b_gate * conv, cg_ref[...]).astype(BF16)

    norm(0)
    for k in range(n_sub):
        if k + 1 < n_sub:
            norm(k + 1)
        if k >= 1:
            conv_mixer(k - 1)
        project(k)
    conv_mixer(n_sub - 1)


def _in_proj(x2, mod3, norm1_g, w_in_b, conv_w, conv_out_g, seq):
    rows, d = x2.shape
    tm = IN_ROW_TILE
    tiles_per_seq = seq // tm
    n_in = w_in_b.shape[1]
    n_qkv = n_in - 3 * CONV_DIM
    return pl.pallas_call(
        functools.partial(_in_proj_kernel, tiles_per_seq),
        grid=(rows // tm,),
        in_specs=[pl.BlockSpec((tm, d), lambda i: (i, 0)),
                  pl.BlockSpec((None, N_MOD, d), lambda i: (i // tiles_per_seq, 0, 0)),
                  _const_spec((1, d)),
                  _const_spec((d, n_in)),
                  _const_spec(conv_w.shape),
                  _const_spec((1, CONV_DIM))],
        out_specs=[pl.BlockSpec((tm, CONV_DIM), lambda i: (i, 0)),
                   pl.BlockSpec((tm, n_qkv), lambda i: (i, 0))],
        out_shape=[jax.ShapeDtypeStruct((rows, CONV_DIM), BF16),
                   jax.ShapeDtypeStruct((rows, n_qkv), F32)],
        scratch_shapes=[pltpu.VMEM((tm + 8, CONV_DIM), F32),
                        pltpu.VMEM((tm, d), BF16),
                        pltpu.VMEM((2, IN_SUB_ROWS, 3 * CONV_DIM), F32)],
        compiler_params=pltpu.CompilerParams(dimension_semantics=("arbitrary",),
                                             vmem_limit_bytes=VMEM_LIMIT),
        name="in_proj",
    )(x2, mod3, norm1_g.reshape(1, d), w_in_b, conv_w, conv_out_g.reshape(1, CONV_DIM))


def _head_rms(a, g, hsum):
    s_hi, s_lo = _split_bf16(a * a)
    ssq = jnp.dot(jnp.concatenate([s_hi, s_lo], axis=1), hsum, preferred_element_type=F32)
    return a * lax.rsqrt(ssq * (1.0 / SB_HEAD_DIM) + EPS) * g


def _sb_attn_kernel(q_ref, k_ref, v_ref, qg_ref, kg_ref, tri_ref, hsum_ref, o_ref,
                    kbd_ref, vbd_ref, qn_ref, acc_ref, carry_ref,
                    z_ref, lb_ref, lst_ref, pre_ref, tot_ref):
    t = ATT_TILE
    sub = Q_TILE // t
    n_ktiles = k_ref.shape[0] // t
    n_qtiles = q_ref.shape[0] // Q_TILE
    lo_lane = lax.broadcasted_iota(jnp.int32, (t, LANES), 1) < SB_HEAD_DIM
    lo_row = lax.broadcasted_iota(jnp.int32, (LANES, t), 0) < SB_HEAD_DIM
    row = lax.broadcasted_iota(jnp.int32, (t, 2 * t), 0)
    col = lax.broadcasted_iota(jnp.int32, (t, 2 * t), 1) & (t - 1)
    causal = col < row

    def prep(c, _):
        for i in range(PREP_TILES):
            j = c * PREP_TILES + i
            r0 = pl.multiple_of(j * t, t)
            kt = _head_rms(k_ref[pl.ds(r0, t), :], kg_ref[...], hsum_ref[...]).T
            kbd_ref[j, :, :t] = jnp.where(lo_row, kt, 0.0).astype(BF16)
            kbd_ref[j, :, t:] = jnp.where(lo_row, 0.0, kt).astype(BF16)
            v = v_ref[pl.ds(r0, t), :]
            vbd_ref[j, :t, :] = jnp.where(lo_lane, v, 0.0).astype(BF16)
            vbd_ref[j, t:, :] = jnp.where(lo_lane, 0.0, v).astype(BF16)
            qn = _head_rms(q_ref[pl.ds(r0, t), :], qg_ref[...], hsum_ref[...])
            qn_ref[pl.ds(r0, t), :] = (qn * (SB_HEAD_DIM ** -0.5 * LOG2E)).astype(BF16)
        return 0

    lax.fori_loop(0, n_ktiles // PREP_TILES, prep, 0)


    def qk(step):
        j, d, _, q0 = step
        q = qn_ref[pl.ds(q0 + d * t, Q_TILE - d * t), :]
        z_ref[d * t:, :] = jnp.dot(q, kbd_ref[j], preferred_element_type=F32)

    def scores(step):
        _, d, diag, _ = step
        r0 = d * t
        n = Q_TILE - r0
        z = z_ref[r0:, :]
        sp = jnp.log2(1.0 + jnp.exp2(-jnp.abs(z)))
        log_beta = jnp.minimum(z, 0.0) - sp
        log_1mb = log_beta - z
        if diag:
            masked = jnp.where(causal, log_1mb[:t], 0.0)
            log_1mb = masked if n == t else jnp.concatenate([masked, log_1mb[t:]], axis=0)
        lb_ref[r0:, :] = log_beta
        l16 = log_1mb.astype(BF16)
        lst_ref[0, r0:, :] = l16[:, :t]
        lst_ref[1, r0:, :] = l16[:, t:]

    def cumsum(step):
        r0 = step[1] * t
        for h in range(2):
            c = h * t
            cum = jnp.dot(lst_ref[h, r0:, :], tri_ref[...], preferred_element_type=F32)
            pre_ref[r0:, c:c + t] = lb_ref[r0:, c:c + t] + cum[:, :t]
            tot_ref[r0:, c:c + t] = cum[:, t:]

    def weights(step):
        j, d, diag, _ = step
        r0 = d * t
        n = Q_TILE - r0
        if diag:
            a = jnp.where(causal, jnp.exp2(pre_ref[r0:r0 + t, :]), 0.0)
            carry_ref[r0:r0 + t, :] = tot_ref[r0:r0 + t, :]
            if n > t:
                carry = carry_ref[r0 + t:, :]
                a = jnp.concatenate([a, jnp.exp2(pre_ref[r0 + t:, :] + carry)], axis=0)
                carry_ref[r0 + t:, :] = carry + tot_ref[r0 + t:, :]
        else:
            carry = carry_ref[...]
            a = jnp.exp2(pre_ref[...] + carry)
            carry_ref[...] = carry + tot_ref[...]
        av = jnp.dot(a.astype(BF16), vbd_ref[j], preferred_element_type=F32)
        if diag:
            acc_ref[r0:r0 + t, :] = av[:t]
            if n > t:
                acc_ref[r0 + t:, :] += av[t:]
        else:
            acc_ref[...] += av

    stages = (qk, scores, cumsum, weights)
    n_stages = len(stages)

    steps = []
    for qi in range(n_qtiles):
        jd = qi * sub
        tile_steps = [(jd + d, d, True, qi * Q_TILE) for d in reversed(range(sub))]
        tile_steps += [(j, 0, False, qi * Q_TILE) for j in reversed(range(jd))]
        steps += [(st, i == len(tile_steps) - 1) for i, st in enumerate(tile_steps)]
    for g in range(len(steps) + n_stages - 1):
        for s in reversed(range(n_stages)):
            if 0 <= g - s < len(steps):
                st, last_of_tile = steps[g - s]
                stages[s](st)
                if last_of_tile and s == n_stages - 1:
                    o_ref[st[3]:st[3] + Q_TILE, :] = acc_ref[...]


def _sb_attn(qkv, q_norm_g, k_norm_g, bsz, seq):
    rows = qkv.shape[0]
    t = ATT_TILE
    n_pairs = SB_DIM // LANES
    n_tiles = seq // t
    j_idx = jnp.arange(t)[:, None]
    s_idx = jnp.arange(2 * t)[None, :]
    tri = ((s_idx >= t) | (j_idx > s_idx)).astype(BF16)
    lane_head = jnp.arange(LANES) // SB_HEAD_DIM
    hsum = (jnp.tile(lane_head, 2)[:, None] == lane_head[None, :]).astype(BF16)
    reps = LANES // SB_HEAD_DIM
    qg = jnp.tile(q_norm_g, reps).reshape(1, LANES)
    kg = jnp.tile(k_norm_g, reps).reshape(1, LANES)
    return pl.pallas_call(
        _sb_attn_kernel,
        grid=(bsz, n_pairs),
        in_specs=[pl.BlockSpec((seq, LANES), lambda b, p: (b, p)),
                  pl.BlockSpec((seq, LANES), lambda b, p: (b, n_pairs + p)),
                  pl.BlockSpec((seq, LANES), lambda b, p: (b, 2 * n_pairs + p)),
                  _const_spec((1, LANES)),
                  _const_spec((1, LANES)),
                  _const_spec((t, 2 * t)),
                  _const_spec((2 * LANES, LANES))],
        out_specs=pl.BlockSpec((seq, LANES), lambda b, p: (b, p)),
        out_shape=jax.ShapeDtypeStruct((rows, SB_DIM), F32),
        scratch_shapes=[pltpu.VMEM((n_tiles, LANES, 2 * t), BF16),
                        pltpu.VMEM((n_tiles, 2 * t, LANES), BF16),
                        pltpu.VMEM((seq, LANES), BF16),
                        pltpu.VMEM((Q_TILE, LANES), F32),
                        pltpu.VMEM((Q_TILE, 2 * t), F32),
                        pltpu.VMEM((Q_TILE, 2 * t), F32),
                        pltpu.VMEM((Q_TILE, 2 * t), F32),
                        pltpu.VMEM((2, Q_TILE, t), BF16),
                        pltpu.VMEM((Q_TILE, 2 * t), F32),
                        pltpu.VMEM((Q_TILE, 2 * t), F32)],
        compiler_params=pltpu.CompilerParams(dimension_semantics=("parallel", "parallel"),
                                             vmem_limit_bytes=VMEM_LIMIT),
        name="sb_attn",
    )(qkv, qkv, qkv, qg, kg, tri, hsum)


def _mix_ffn_kernel(mix_ref, ya_ref, x_ref, mod_ref, ag_ref, wo_ref, g2_ref, w1_ref, w2_ref,
                    o_ref):
    tm = x_ref.shape[0]
    d_ff = w1_ref.shape[1]
    for r0 in range(0, tm, FFN_SUB_ROWS):
        rs = slice(r0, r0 + FFN_SUB_ROWS)
        yb = _rms(ya_ref[rs, :], ag_ref[...]).astype(BF16)
        o = (jnp.dot(mix_ref[rs, :], wo_ref[:CONV_DIM, :], preferred_element_type=F32)
             + jnp.dot(yb, wo_ref[CONV_DIM:, :], preferred_element_type=F32))
        x1 = x_ref[rs, :] + mod_ref[2:3, :] * o
        h2 = (_rms(x1, g2_ref[...]) * (1.0 + mod_ref[4:5, :]) + mod_ref[3:4, :]).astype(BF16)
        f = None
        for c0 in range(0, d_ff, FFN_CHUNK):
            u = jnp.dot(h2, w1_ref[:, c0:c0 + FFN_CHUNK], preferred_element_type=F32)
            r = jnp.maximum(u, 0.0)
            fc = jnp.dot((r * r).astype(BF16), w2_ref[c0:c0 + FFN_CHUNK, :],
                         preferred_element_type=F32)
            f = fc if f is None else f + fc
        o_ref[rs, :] = x1 + mod_ref[5:6, :] * f


def _mix_ffn(mix_a, y_attn, x2, mod3, attn_out_g, w_out_b, norm2_g, w1_b, w2_b, seq):
    rows, d = x2.shape
    tm = ROW_TILE
    tiles_per_seq = seq // tm
    return pl.pallas_call(
        _mix_ffn_kernel,
        grid=(rows // tm,),
        in_specs=[pl.BlockSpec((tm, CONV_DIM), lambda i: (i, 0)),
                  pl.BlockSpec((tm, SB_DIM), lambda i: (i, 0)),
                  pl.BlockSpec((tm, d), lambda i: (i, 0)),
                  pl.BlockSpec((None, N_MOD, d), lambda i: (i // tiles_per_seq, 0, 0)),
                  _const_spec((1, SB_DIM)),
                  _const_spec(w_out_b.shape),
                  _const_spec((1, d)),
                  _const_spec(w1_b.shape),
                  _const_spec(w2_b.shape)],
        out_specs=pl.BlockSpec((tm, d), lambda i: (i, 0)),
        out_shape=jax.ShapeDtypeStruct((rows, d), F32),
        compiler_params=pltpu.CompilerParams(dimension_semantics=("parallel",),
                                             vmem_limit_bytes=VMEM_LIMIT),
        name="mix_ffn",
    )(mix_a, y_attn, x2, mod3, attn_out_g.reshape(1, SB_DIM), w_out_b, norm2_g.reshape(1, d),
      w1_b, w2_b)


def kernel(x, c, w_ada, b_ada, norm1_g, w_in, conv_w, q_norm_g, k_norm_g, conv_out_g,
           attn_out_g, w_out, norm2_g, w_ff1, w_ff2):
    bsz, seq, d = x.shape
    assert seq % ROW_TILE == 0 and seq % Q_TILE == 0 and Q_TILE % ATT_TILE == 0
    assert w_in.shape[1] == 3 * CONV_DIM + 3 * SB_DIM
    x2 = x.reshape(bsz * seq, d)
    mod3 = _adaln(c, w_ada, b_ada).reshape(bsz, N_MOD, d)
    mix_a, qkv = _in_proj(x2, mod3, norm1_g, w_in.astype(BF16), conv_w, conv_out_g, seq)
    y_attn = _sb_attn(qkv, q_norm_g, k_norm_g, bsz, seq)
    out = _mix_ffn(mix_a, y_attn, x2, mod3, attn_out_g, w_out.astype(BF16), norm2_g,
                   w_ff1.astype(BF16), w_ff2.astype(BF16), seq)
    return out.reshape(bsz, seq, d)
```

```python
import functools

import jax
import jax.numpy as jnp
from jax import lax
from jax.experimental import pallas as pl
from jax.experimental.pallas import tpu as pltpu

EPS = 1e-6
CONV_DIM = 512
SB_DIM = 512
SB_HEAD_DIM = 64
N_MOD = 6
LANES = 128
ATT_TILE = 128
Q_TILE = 512
PREP_TILES = 4
LOG2E = 1.4426950408889634
ROW_TILE = 512
IN_ROW_TILE = 1024
IN_SUB_ROWS = 256
FFN_CHUNK = 512
VMEM_LIMIT = 56 * 1024 * 1024

F32 = jnp.float32
BF16 = jnp.bfloat16


def _split_bf16(a):
    hi = a.astype(BF16)
    lo = (a - hi.astype(F32)).astype(BF16)
    return hi, lo


def _const_spec(shape):
    return pl.BlockSpec(shape, lambda *_: (0,) * len(shape), pipeline_mode=pl.Buffered(1))


def _adaln_kernel(c_ref, w_ref, b_ref, o_ref):
    c = c_ref[...]
    s = c * (1.0 / (1.0 + jnp.exp(-c)))
    s_hi, s_lo = _split_bf16(s)
    w_hi, w_lo = _split_bf16(w_ref[...])
    dot = functools.partial(jnp.dot, preferred_element_type=F32)
    o_ref[...] = dot(s_hi, w_hi) + (dot(s_hi, w_lo) + dot(s_lo, w_hi)) + b_ref[...]


def _adaln(c, w_ada, b_ada):
    bsz, d = c.shape
    n = w_ada.shape[1]
    tn = 1024
    return pl.pallas_call(
        _adaln_kernel,
        grid=(n // tn,),
        in_specs=[pl.BlockSpec((bsz, d), lambda j: (0, 0)),
                  pl.BlockSpec((d, tn), lambda j: (0, j)),
                  pl.BlockSpec((1, tn), lambda j: (0, j))],
        out_specs=pl.BlockSpec((bsz, tn), lambda j: (0, j)),
        out_shape=jax.ShapeDtypeStruct((bsz, n), F32),
        compiler_params=pltpu.CompilerParams(dimension_semantics=("parallel",),
                                             vmem_limit_bytes=VMEM_LIMIT),
        name="adaln",
    )(c, w_ada, b_ada.reshape(1, n))


def _rms(xf, g):
    ms = jnp.mean(xf * xf, axis=-1, keepdims=True)
    return xf * lax.rsqrt(ms + EPS) * g


def _in_proj_kernel(tiles_per_seq, x_ref, mod_ref, g1_ref, w_ref, cw_ref, cg_ref,
                    mix_ref, qkv_ref, ext_ref, hb_ref, p_ref):
    tm = x_ref.shape[0]
    i = pl.program_id(0)

    @pl.when(i % tiles_per_seq == 0)
    def _():
        ext_ref[0:8, :] = jnp.zeros((8, CONV_DIM), F32)

    @pl.when(i % tiles_per_seq != 0)
    def _():
        ext_ref[0:8, :] = ext_ref[tm:tm + 8, :]

    shift = mod_ref[0:1, :]
    scale = mod_ref[1:2, :]
    n_conv = 3 * CONV_DIM
    n_sub = tm // IN_SUB_ROWS

    def rows(k):
        return slice(k * IN_SUB_ROWS, (k + 1) * IN_SUB_ROWS)

    def norm(k):
        h = _rms(x_ref[rows(k), :], g1_ref[...]) * (1.0 + scale) + shift
        hb_ref[rows(k), :] = h.astype(BF16)

    def project(k):
        hb = hb_ref[rows(k), :]
        p_ref[k % 2] = jnp.dot(hb, w_ref[:, :n_conv], preferred_element_type=F32)
        qkv_ref[rows(k), :] = jnp.dot(hb, w_ref[:, n_conv:], preferred_element_type=F32)

    def conv_mixer(k):
        r0 = k * IN_SUB_ROWS
        r1 = r0 + IN_SUB_ROWS
        b_gate = p_ref[k % 2, :, :CONV_DIM]
        cu = p_ref[k % 2, :, CONV_DIM:2 * CONV_DIM] * p_ref[k % 2, :, 2 * CONV_DIM:]
        ext_ref[r0 + 8:r1 + 8, :] = cu
        conv = (cw_ref[2:3, :] * cu + cw_ref[1:2, :] * ext_ref[r0 + 7:r1 + 7, :]
                + cw_ref[0:1, :] * ext_ref[r0 + 6:r1 + 6, :])
        mix_ref[rows(k), :] = _rms(b_gate * conv, cg_ref[...]).astype(BF16)

    norm(0)
    for k in range(n_sub):
        if k + 1 < n_sub:
            norm(k + 1)
        if k >= 1:
            conv_mixer(k - 1)
        project(k)
    conv_mixer(n_sub - 1)


def _in_proj(x2, mod3, norm1_g, w_in_b, conv_w, conv_out_g, seq):
    rows, d = x2.shape
    tm = IN_ROW_TILE
    tiles_per_seq = seq // tm
    n_in = w_in_b.shape[1]
    n_qkv = n_in - 3 * CONV_DIM
    return pl.pallas_call(
        functools.partial(_in_proj_kernel, tiles_per_seq),
        grid=(rows // tm,),
        in_specs=[pl.BlockSpec((tm, d), lambda i: (i, 0)),
                  pl.BlockSpec((None, N_MOD, d), lambda i: (i // tiles_per_seq, 0, 0)),
                  _const_spec((1, d)),
                  _const_spec((d, n_in)),
                  _const_spec(conv_w.shape),
                  _const_spec((1, CONV_DIM))],
        out_specs=[pl.BlockSpec((tm, CONV_DIM), lambda i: (i, 0)),
                   pl.BlockSpec((tm, n_qkv), lambda i: (i, 0))],
        out_shape=[jax.ShapeDtypeStruct((rows, CONV_DIM), BF16),
                   jax.ShapeDtypeStruct((rows, n_qkv), F32)],
        scratch_shapes=[pltpu.VMEM((tm + 8, CONV_DIM), F32),
                        pltpu.VMEM((tm, d), BF16),
                        pltpu.VMEM((2, IN_SUB_ROWS, 3 * CONV_DIM), F32)],
        compiler_params=pltpu.CompilerParams(dimension_semantics=("arbitrary",),
                                             vmem_limit_bytes=VMEM_LIMIT),
        name="in_proj",
    )(x2, mod3, norm1_g.reshape(1, d), w_in_b, conv_w, conv_out_g.reshape(1, CONV_DIM))


def _head_rms(a, g, hsum):
    s_hi, s_lo = _split_bf16(a * a)
    ssq = jnp.dot(jnp.concatenate([s_hi, s_lo], axis=1), hsum, preferred_element_type=F32)
    return a * lax.rsqrt(ssq * (1.0 / SB_HEAD_DIM) + EPS) * g


def _sb_attn_kernel(q_ref, k_ref, v_ref, qg_ref, kg_ref, tri_ref, hsum_ref, o_ref,
                    kbd_ref, vbd_ref, qt_ref, acc_ref, carry_ref,
                    z_ref, rhs_ref, pre_ref, tot_ref):
    t = ATT_TILE
    sub = Q_TILE // t
    n_ktiles = k_ref.shape[0] // t
    n_qtiles = q_ref.shape[0] // Q_TILE
    lo_lane = lax.broadcasted_iota(jnp.int32, (t, LANES), 1) < SB_HEAD_DIM
    lo_row = lax.broadcasted_iota(jnp.int32, (LANES, t), 0) < SB_HEAD_DIM
    key = lax.broadcasted_iota(jnp.int32, (2 * t, t), 0) & (t - 1)
    qry = lax.broadcasted_iota(jnp.int32, (2 * t, t), 1)
    causal = key < qry

    def prep(c, _):
        for i in range(PREP_TILES):
            j = c * PREP_TILES + i
            r0 = pl.multiple_of(j * t, t)
            kn = _head_rms(k_ref[pl.ds(r0, t), :], kg_ref[...], hsum_ref[...])
            kbd_ref[j, :t, :] = jnp.where(lo_lane, kn, 0.0).astype(BF16)
            kbd_ref[j, t:, :] = jnp.where(lo_lane, 0.0, kn).astype(BF16)
            vt = v_ref[pl.ds(r0, t), :].T
            vbd_ref[j, :, :t] = jnp.where(lo_row, vt, 0.0).astype(BF16)
            vbd_ref[j, :, t:] = jnp.where(lo_row, 0.0, vt).astype(BF16)
            qn = _head_rms(q_ref[pl.ds(r0, t), :], qg_ref[...], hsum_ref[...])
            qt_ref[j] = (qn * (SB_HEAD_DIM ** -0.5 * LOG2E)).T.astype(BF16)
        return 0

    lax.fori_loop(0, n_ktiles // PREP_TILES, prep, 0)


    def qk(step):
        j, d, _, qi, slot = step
        qt = jnp.concatenate([qt_ref[qi * sub + r] for r in range(d, sub)], axis=1)
        z = jnp.dot(kbd_ref[j], qt, preferred_element_type=F32)
        z_ref[slot, :, d * t:] = z.astype(BF16)

    def scores(step):
        _, d, diag, _, slot = step
        c0 = d * t
        z = z_ref[slot, :, c0:]
        sp = jnp.log2((1.0 + jnp.exp2(-jnp.abs(z))).astype(F32)).astype(BF16)
        log_beta = jnp.minimum(z, 0.0) - sp
        log_1mb = log_beta - z
        if diag:
            masked = jnp.where(causal, log_1mb[:, :t], 0.0)
            log_1mb = (masked if c0 + t == Q_TILE
                       else jnp.concatenate([masked, log_1mb[:, t:]], axis=1))
        for h in range(2):
            rows = slice(h * t, (h + 1) * t)
            rhs_ref[slot, h, :t, c0:] = log_1mb[rows, :]
            rhs_ref[slot, h, t:, c0:] = log_beta[rows, :]

    def cumsum(step):
        c0 = step[1] * t
        slot = step[4]
        for h in range(2):
            rhs = rhs_ref[slot, h, :, c0:]
            pre = jnp.dot(tri_ref[...], rhs, preferred_element_type=F32)
            pre_ref[slot, h * t:(h + 1) * t, c0:] = pre
            key0 = rhs[0:1, :].astype(F32) - rhs[t:t + 1, :].astype(F32)
            tot_ref[slot, h:h + 1, c0:] = pre[0:1, :] + key0

    def per_key(x):
        return jnp.concatenate([jnp.broadcast_to(x[h:h + 1, :], (t, x.shape[1]))
                                for h in range(2)], axis=0)

    def weights(step):
        j, d, diag, _, slot = step
        c0 = d * t
        pre_s, tot_s = pre_ref.at[slot], tot_ref.at[slot]
        if diag:
            a = jnp.where(causal, jnp.exp2(pre_s[:, c0:c0 + t]), 0.0)
            carry_ref[:, c0:c0 + t] = tot_s[:, c0:c0 + t]
            if c0 + t < Q_TILE:
                carry = carry_ref[:, c0 + t:]
                a = jnp.concatenate(
                    [a, jnp.exp2(pre_s[:, c0 + t:] + per_key(carry))], axis=1)
                carry_ref[:, c0 + t:] = carry + tot_s[:, c0 + t:]
        else:
            carry = carry_ref[...]
            a = jnp.exp2(pre_s[...] + per_key(carry))
            carry_ref[...] = carry + tot_s[...]
        av = jnp.dot(vbd_ref[j], a.astype(BF16), preferred_element_type=F32)
        if diag:
            acc_ref[:, c0:c0 + t] = av[:, :t]
            if c0 + t < Q_TILE:
                acc_ref[:, c0 + t:] += av[:, t:]
        else:
            acc_ref[...] += av

    stages = (qk, scores, cumsum, weights)
    n_stages = len(stages)

    steps = []
    for qi in range(n_qtiles):
        jd = qi * sub
        tile_steps = [(jd + d, d, True, qi) for d in reversed(range(sub))]
        tile_steps += [(j, 0, False, qi) for j in reversed(range(jd))]
        steps += [(st + ((len(steps) + i) % 2,), i == len(tile_steps) - 1)
                  for i, st in enumerate(tile_steps)]
    for g in range(len(steps) + n_stages - 1):
        for s in reversed(range(n_stages)):
            if 0 <= g - s < len(steps):
                st, last_of_tile = steps[g - s]
                stages[s](st)
                if last_of_tile and s == n_stages - 1:
                    for r in range(sub):
                        q0 = st[3] * Q_TILE + r * t
                        o_ref[q0:q0 + t, :] = acc_ref[:, r * t:(r + 1) * t].T


def _sb_attn(qkv, q_norm_g, k_norm_g, bsz, seq):
    rows = qkv.shape[0]
    t = ATT_TILE
    n_pairs = SB_DIM // LANES
    n_tiles = seq // t
    upper = jnp.arange(t)[None, :] > jnp.arange(t)[:, None]
    tri = jnp.concatenate([upper, jnp.eye(t, dtype=bool)], axis=1).astype(BF16)
    lane_head = jnp.arange(LANES) // SB_HEAD_DIM
    hsum = (jnp.tile(lane_head, 2)[:, None] == lane_head[None, :]).astype(BF16)
    reps = LANES // SB_HEAD_DIM
    qg = jnp.tile(q_norm_g, reps).reshape(1, LANES)
    kg = jnp.tile(k_norm_g, reps).reshape(1, LANES)
    return pl.pallas_call(
        _sb_attn_kernel,
        grid=(bsz, n_pairs),
        in_specs=[pl.BlockSpec((seq, LANES), lambda b, p: (b, p)),
                  pl.BlockSpec((seq, LANES), lambda b, p: (b, n_pairs + p)),
                  pl.BlockSpec((seq, LANES), lambda b, p: (b, 2 * n_pairs + p)),
                  _const_spec((1, LANES)),
                  _const_spec((1, LANES)),
                  _const_spec((t, 2 * t)),
                  _const_spec((2 * LANES, LANES))],
        out_specs=pl.BlockSpec((seq, LANES), lambda b, p: (b, p)),
        out_shape=jax.ShapeDtypeStruct((rows, SB_DIM), F32),
        scratch_shapes=[pltpu.VMEM((n_tiles, 2 * t, LANES), BF16),
                        pltpu.VMEM((n_tiles, LANES, 2 * t), BF16),
                        pltpu.VMEM((n_tiles, LANES, t), BF16),
                        pltpu.VMEM((LANES, Q_TILE), F32),
                        pltpu.VMEM((2, Q_TILE), F32),
                        pltpu.VMEM((2, 2 * t, Q_TILE), BF16),
                        pltpu.VMEM((2, 2, 2 * t, Q_TILE), BF16),
                        pltpu.VMEM((2, 2 * t, Q_TILE), F32),
                        pltpu.VMEM((2, 2, Q_TILE), F32)],
        compiler_params=pltpu.CompilerParams(dimension_semantics=("parallel", "parallel"),
                                             vmem_limit_bytes=VMEM_LIMIT),
        name="sb_attn",
    )(qkv, qkv, qkv, qg, kg, tri, hsum)


def _mix_ffn_kernel(mix_ref, ya_ref, x_ref, mod_ref, ag_ref, wo_ref, g2_ref, w1_ref, w2_ref,
                    o_ref):
    d_ff = w1_ref.shape[1]
    yb = _rms(ya_ref[...], ag_ref[...]).astype(BF16)
    o = (jnp.dot(mix_ref[...], wo_ref[:CONV_DIM, :], preferred_element_type=F32)
         + jnp.dot(yb, wo_ref[CONV_DIM:, :], preferred_element_type=F32))
    x1 = x_ref[...] + mod_ref[2:3, :] * o
    h2 = (_rms(x1, g2_ref[...]) * (1.0 + mod_ref[4:5, :]) + mod_ref[3:4, :]).astype(BF16)
    f = None
    for c0 in range(0, d_ff, FFN_CHUNK):
        u = jnp.dot(h2, w1_ref[:, c0:c0 + FFN_CHUNK], preferred_element_type=F32)
        r = jnp.maximum(u, 0.0)
        fc = jnp.dot((r * r).astype(BF16), w2_ref[c0:c0 + FFN_CHUNK, :],
                     preferred_element_type=F32)
        f = fc if f is None else f + fc
    o_ref[...] = x1 + mod_ref[5:6, :] * f


def _mix_ffn(mix_a, y_attn, x2, mod3, attn_out_g, w_out_b, norm2_g, w1_b, w2_b, seq):
    rows, d = x2.shape
    tm = ROW_TILE
    tiles_per_seq = seq // tm
    return pl.pallas_call(
        _mix_ffn_kernel,
        grid=(rows // tm,),
        in_specs=[pl.BlockSpec((tm, CONV_DIM), lambda i: (i, 0)),
                  pl.BlockSpec((tm, SB_DIM), lambda i: (i, 0)),
                  pl.BlockSpec((tm, d), lambda i: (i, 0)),
                  pl.BlockSpec((None, N_MOD, d), lambda i: (i // tiles_per_seq, 0, 0)),
                  _const_spec((1, SB_DIM)),
                  _const_spec(w_out_b.shape),
                  _const_spec((1, d)),
                  _const_spec(w1_b.shape),
                  _const_spec(w2_b.shape)],
        out_specs=pl.BlockSpec((tm, d), lambda i: (i, 0)),
        out_shape=jax.ShapeDtypeStruct((rows, d), F32),
        compiler_params=pltpu.CompilerParams(dimension_semantics=("parallel",),
                                             vmem_limit_bytes=VMEM_LIMIT),
        name="mix_ffn",
    )(mix_a, y_attn, x2, mod3, attn_out_g.reshape(1, SB_DIM), w_out_b, norm2_g.reshape(1, d),
      w1_b, w2_b)


def kernel(x, c, w_ada, b_ada, norm1_g, w_in, conv_w, q_norm_g, k_norm_g, conv_out_g,
           attn_out_g, w_out, norm2_g, w_ff1, w_ff2):
    bsz, seq, d = x.shape
    assert seq % IN_ROW_TILE == 0 and seq % ROW_TILE == 0 and seq % Q_TILE == 0
    assert Q_TILE % ATT_TILE == 0 and (seq // ATT_TILE) % PREP_TILES == 0
    assert w_in.shape[1] == 3 * CONV_DIM + 3 * SB_DIM
    x2 = x.reshape(bsz * seq, d)
    mod3 = _adaln(c, w_ada, b_ada).reshape(bsz, N_MOD, d)
    mix_a, qkv = _in_proj(x2, mod3, norm1_g, w_in.astype(BF16), conv_w, conv_out_g, seq)
    y_attn = _sb_attn(qkv, q_norm_g, k_norm_g, bsz, seq)
    out = _mix_ffn(mix_a, y_attn, x2, mod3, attn_out_g, w_out.astype(BF16), norm2_g,
                   w_ff1.astype(BF16), w_ff2.astype(BF16), seq)
    return out.reshape(bsz, seq, d)
```

```python
import functools

import jax
import jax.numpy as jnp
from jax import lax
from jax.experimental import pallas as pl
from jax.experimental.pallas import tpu as pltpu

EPS = 1e-6
CONV_DIM = 512
SB_DIM = 512
SB_HEAD_DIM = 64
N_MOD = 6
LANES = 128
ATT_TILE = 128
Q_TILE = 512
PREP_TILES = 4
LOG2E = 1.4426950408889634
ROW_TILE = 512
IN_ROW_TILE = 1024
IN_SUB_ROWS = 256
FFN_CHUNK = 512
VMEM_LIMIT = 56 * 1024 * 1024

F32 = jnp.float32
BF16 = jnp.bfloat16


def _split_bf16(a):
    hi = a.astype(BF16)
    lo = (a - hi.astype(F32)).astype(BF16)
    return hi, lo


def _const_spec(shape):
    return pl.BlockSpec(shape, lambda *_: (0,) * len(shape), pipeline_mode=pl.Buffered(1))


def _adaln_kernel(c_ref, w_ref, b_ref, o_ref):
    c = c_ref[...]
    s = c * (1.0 / (1.0 + jnp.exp(-c)))
    s_hi, s_lo = _split_bf16(s)
    w_hi, w_lo = _split_bf16(w_ref[...])
    dot = functools.partial(jnp.dot, preferred_element_type=F32)
    o_ref[...] = dot(s_hi, w_hi) + (dot(s_hi, w_lo) + dot(s_lo, w_hi)) + b_ref[...]


def _adaln(c, w_ada, b_ada):
    bsz, d = c.shape
    n = w_ada.shape[1]
    tn = 1024
    return pl.pallas_call(
        _adaln_kernel,
        grid=(n // tn,),
        in_specs=[pl.BlockSpec((bsz, d), lambda j: (0, 0)),
                  pl.BlockSpec((d, tn), lambda j: (0, j)),
                  pl.BlockSpec((1, tn), lambda j: (0, j))],
        out_specs=pl.BlockSpec((bsz, tn), lambda j: (0, j)),
        out_shape=jax.ShapeDtypeStruct((bsz, n), F32),
        compiler_params=pltpu.CompilerParams(dimension_semantics=("parallel",),
                                             vmem_limit_bytes=VMEM_LIMIT),
        name="adaln",
    )(c, w_ada, b_ada.reshape(1, n))


def _rms(xf, g):
    ms = jnp.mean(xf * xf, axis=-1, keepdims=True)
    return xf * lax.rsqrt(ms + EPS) * g


def _in_proj_kernel(tiles_per_seq, x_ref, mod_ref, g1_ref, w_ref, cw_ref, cg_ref,
                    mix_ref, qkv_ref, ext_ref, hb_ref, p_ref):
    tm = x_ref.shape[0]
    i = pl.program_id(0)

    @pl.when(i % tiles_per_seq == 0)
    def _():
        ext_ref[0:8, :] = jnp.zeros((8, CONV_DIM), F32)

    @pl.when(i % tiles_per_seq != 0)
    def _():
        ext_ref[0:8, :] = ext_ref[tm:tm + 8, :]

    shift = mod_ref[0:1, :]
    scale = mod_ref[1:2, :]
    n_conv = 3 * CONV_DIM
    n_sub = tm // IN_SUB_ROWS

    def rows(k):
        return slice(k * IN_SUB_ROWS, (k + 1) * IN_SUB_ROWS)

    def norm(k):
        h = _rms(x_ref[rows(k), :], g1_ref[...]) * (1.0 + scale) + shift
        hb_ref[rows(k), :] = h.astype(BF16)

    def project(k):
        hb = hb_ref[rows(k), :]
        p_ref[k % 2] = jnp.dot(hb, w_ref[:, :n_conv], preferred_element_type=F32)
        qkv_ref[rows(k), :] = jnp.dot(hb, w_ref[:, n_conv:], preferred_element_type=F32)

    def conv_mixer(k):
        r0 = k * IN_SUB_ROWS
        r1 = r0 + IN_SUB_ROWS
        b_gate = p_ref[k % 2, :, :CONV_DIM]
        cu = p_ref[k % 2, :, CONV_DIM:2 * CONV_DIM] * p_ref[k % 2, :, 2 * CONV_DIM:]
        ext_ref[r0 + 8:r1 + 8, :] = cu
        conv = (cw_ref[2:3, :] * cu + cw_ref[1:2, :] * ext_ref[r0 + 7:r1 + 7, :]
                + cw_ref[0:1, :] * ext_ref[r0 + 6:r1 + 6, :])
        mix_ref[rows(k), :] = _rms(b_gate * conv, cg_ref[...]).astype(BF16)

    norm(0)
    for k in range(n_sub):
        if k + 1 < n_sub:
            norm(k + 1)
        if k >= 1:
            conv_mixer(k - 1)
        project(k)
    conv_mixer(n_sub - 1)


def _in_proj(x2, mod3, norm1_g, w_in_b, conv_w, conv_out_g, seq):
    rows, d = x2.shape
    tm = IN_ROW_TILE
    tiles_per_seq = seq // tm
    n_in = w_in_b.shape[1]
    n_qkv = n_in - 3 * CONV_DIM
    return pl.pallas_call(
        functools.partial(_in_proj_kernel, tiles_per_seq),
        grid=(rows // tm,),
        in_specs=[pl.BlockSpec((tm, d), lambda i: (i, 0)),
                  pl.BlockSpec((None, N_MOD, d), lambda i: (i // tiles_per_seq, 0, 0)),
                  _const_spec((1, d)),
                  _const_spec((d, n_in)),
                  _const_spec(conv_w.shape),
                  _const_spec((1, CONV_DIM))],
        out_specs=[pl.BlockSpec((tm, CONV_DIM), lambda i: (i, 0)),
                   pl.BlockSpec((tm, n_qkv), lambda i: (i, 0))],
        out_shape=[jax.ShapeDtypeStruct((rows, CONV_DIM), BF16),
                   jax.ShapeDtypeStruct((rows, n_qkv), F32)],
        scratch_shapes=[pltpu.VMEM((tm + 8, CONV_DIM), F32),
                        pltpu.VMEM((tm, d), BF16),
                        pltpu.VMEM((2, IN_SUB_ROWS, 3 * CONV_DIM), F32)],
        compiler_params=pltpu.CompilerParams(dimension_semantics=("arbitrary",),
                                             vmem_limit_bytes=VMEM_LIMIT),
        name="in_proj",
    )(x2, mod3, norm1_g.reshape(1, d), w_in_b, conv_w, conv_out_g.reshape(1, CONV_DIM))


def _head_rms(a, g, hsum):
    s_hi, s_lo = _split_bf16(a * a)
    ssq = jnp.dot(jnp.concatenate([s_hi, s_lo], axis=1), hsum, preferred_element_type=F32)
    return a * lax.rsqrt(ssq * (1.0 / SB_HEAD_DIM) + EPS) * g


def _sb_attn_kernel(q_ref, k_ref, v_ref, qg_ref, kg_ref, tri_ref, hsum_ref, o_ref,
                    kbd_ref, vbd_ref, qn_ref, acc_ref, carry_ref,
                    z_ref, lhs_ref, pre_ref, tot_ref):
    t = ATT_TILE
    sub = Q_TILE // t
    n_ktiles = k_ref.shape[0] // t
    n_qtiles = q_ref.shape[0] // Q_TILE
    lo_lane = lax.broadcasted_iota(jnp.int32, (t, LANES), 1) < SB_HEAD_DIM
    lo_row = lax.broadcasted_iota(jnp.int32, (LANES, t), 0) < SB_HEAD_DIM
    row = lax.broadcasted_iota(jnp.int32, (t, 2 * t), 0)
    col = lax.broadcasted_iota(jnp.int32, (t, 2 * t), 1) & (t - 1)
    causal = col < row

    def prep(c, _):
        for i in range(PREP_TILES):
            j = c * PREP_TILES + i
            r0 = pl.multiple_of(j * t, t)
            kt = _head_rms(k_ref[pl.ds(r0, t), :], kg_ref[...], hsum_ref[...]).T
            kbd_ref[j, :, :t] = jnp.where(lo_row, kt, 0.0).astype(BF16)
            kbd_ref[j, :, t:] = jnp.where(lo_row, 0.0, kt).astype(BF16)
            v = v_ref[pl.ds(r0, t), :]
            vbd_ref[j, :t, :] = jnp.where(lo_lane, v, 0.0).astype(BF16)
            vbd_ref[j, t:, :] = jnp.where(lo_lane, 0.0, v).astype(BF16)
            qn = _head_rms(q_ref[pl.ds(r0, t), :], qg_ref[...], hsum_ref[...])
            qn_ref[pl.ds(r0, t), :] = (qn * (SB_HEAD_DIM ** -0.5 * LOG2E)).astype(BF16)
        return 0

    lax.fori_loop(0, n_ktiles // PREP_TILES, prep, 0)


    def qk(step):
        j, d, _, q0, slot = step
        q = qn_ref[pl.ds(q0 + d * t, Q_TILE - d * t), :]
        z = jnp.dot(q, kbd_ref[j], preferred_element_type=F32)
        z_ref[slot, d * t:, :] = z.astype(BF16)

    def scores(step):
        _, d, diag, _, slot = step
        r0 = d * t
        n = Q_TILE - r0
        z = z_ref[slot, r0:, :]
        sp = jnp.log2((1.0 + jnp.exp2(-jnp.abs(z))).astype(F32)).astype(BF16)
        log_beta = jnp.minimum(z, 0.0) - sp
        log_1mb = log_beta - z
        if diag:
            masked = jnp.where(causal, log_1mb[:t], 0.0)
            log_1mb = masked if n == t else jnp.concatenate([masked, log_1mb[t:]], axis=0)
        for h in range(2):
            c = h * t
            lhs_ref[slot, h, r0:, :t] = log_1mb[:, c:c + t]
            lhs_ref[slot, h, r0:, t:] = log_beta[:, c:c + t]

    def cumsum(step):
        r0 = step[1] * t
        slot = step[4]
        for h in range(2):
            c = h * t
            cum = jnp.dot(lhs_ref[slot, h, r0:, :], tri_ref[...], preferred_element_type=F32)
            pre_ref[slot, r0:, c:c + t] = cum[:, :t]
            tot_ref[slot, r0:, c:c + t] = cum[:, t:]

    def weights(step):
        j, d, diag, _, slot = step
        r0 = d * t
        n = Q_TILE - r0
        pre_s, tot_s = pre_ref.at[slot], tot_ref.at[slot]
        if diag:
            a = jnp.where(causal, jnp.exp2(pre_s[r0:r0 + t, :]), 0.0)
            carry_ref[r0:r0 + t, :] = tot_s[r0:r0 + t, :]
            if n > t:
                carry = carry_ref[r0 + t:, :]
                a = jnp.concatenate([a, jnp.exp2(pre_s[r0 + t:, :] + carry)], axis=0)
                carry_ref[r0 + t:, :] = carry + tot_s[r0 + t:, :]
        else:
            carry = carry_ref[...]
            a = jnp.exp2(pre_s[...] + carry)
            carry_ref[...] = carry + tot_s[...]
        av = jnp.dot(a.astype(BF16), vbd_ref[j], preferred_element_type=F32)
        if diag:
            acc_ref[r0:r0 + t, :] = av[:t]
            if n > t:
                acc_ref[r0 + t:, :] += av[t:]
        else:
            acc_ref[...] += av

    stages = (qk, scores, cumsum, weights)
    n_stages = len(stages)

    steps = []
    for qi in range(n_qtiles):
        jd = qi * sub
        tile_steps = [(jd + d, d, True, qi * Q_TILE) for d in reversed(range(sub))]
        tile_steps += [(j, 0, False, qi * Q_TILE) for j in reversed(range(jd))]
        steps += [(st + ((len(steps) + i) % 2,), i == len(tile_steps) - 1)
                  for i, st in enumerate(tile_steps)]
    for g in range(len(steps) + n_stages - 1):
        for s in reversed(range(n_stages)):
            if 0 <= g - s < len(steps):
                st, last_of_tile = steps[g - s]
                stages[s](st)
                if last_of_tile and s == n_stages - 1:
                    o_ref[st[3]:st[3] + Q_TILE, :] = acc_ref[...]


def _sb_attn(qkv, q_norm_g, k_norm_g, bsz, seq):
    rows = qkv.shape[0]
    t = ATT_TILE
    n_pairs = SB_DIM // LANES
    n_tiles = seq // t
    j_idx = jnp.arange(t)[:, None]
    s_idx = jnp.arange(t)[None, :]
    top = jnp.concatenate([j_idx > s_idx, jnp.ones((t, t), bool)], axis=1)
    bottom = jnp.concatenate([jnp.eye(t, dtype=bool), jnp.zeros((t, t), bool)], axis=1)
    tri = jnp.concatenate([top, bottom], axis=0).astype(BF16)
    lane_head = jnp.arange(LANES) // SB_HEAD_DIM
    hsum = (jnp.tile(lane_head, 2)[:, None] == lane_head[None, :]).astype(BF16)
    reps = LANES // SB_HEAD_DIM
    qg = jnp.tile(q_norm_g, reps).reshape(1, LANES)
    kg = jnp.tile(k_norm_g, reps).reshape(1, LANES)
    return pl.pallas_call(
        _sb_attn_kernel,
        grid=(bsz, n_pairs),
        in_specs=[pl.BlockSpec((seq, LANES), lambda b, p: (b, p)),
                  pl.BlockSpec((seq, LANES), lambda b, p: (b, n_pairs + p)),
                  pl.BlockSpec((seq, LANES), lambda b, p: (b, 2 * n_pairs + p)),
                  _const_spec((1, LANES)),
                  _const_spec((1, LANES)),
                  _const_spec((2 * t, 2 * t)),
                  _const_spec((2 * LANES, LANES))],
        out_specs=pl.BlockSpec((seq, LANES), lambda b, p: (b, p)),
        out_shape=jax.ShapeDtypeStruct((rows, SB_DIM), F32),
        scratch_shapes=[pltpu.VMEM((n_tiles, LANES, 2 * t), BF16),
                        pltpu.VMEM((n_tiles, 2 * t, LANES), BF16),
                        pltpu.VMEM((seq, LANES), BF16),
                        pltpu.VMEM((Q_TILE, LANES), F32),
                        pltpu.VMEM((Q_TILE, 2 * t), F32),
                        pltpu.VMEM((2, Q_TILE, 2 * t), BF16),
                        pltpu.VMEM((2, 2, Q_TILE, 2 * t), BF16),
                        pltpu.VMEM((2, Q_TILE, 2 * t), F32),
                        pltpu.VMEM((2, Q_TILE, 2 * t), F32)],
        compiler_params=pltpu.CompilerParams(dimension_semantics=("parallel", "parallel"),
                                             vmem_limit_bytes=VMEM_LIMIT),
        name="sb_attn",
    )(qkv, qkv, qkv, qg, kg, tri, hsum)


def _mix_ffn_kernel(mix_ref, ya_ref, x_ref, mod_ref, ag_ref, wo_ref, g2_ref, w1_ref, w2_ref,
                    o_ref):
    d_ff = w1_ref.shape[1]
    yb = _rms(ya_ref[...], ag_ref[...]).astype(BF16)
    o = (jnp.dot(mix_ref[...], wo_ref[:CONV_DIM, :], preferred_element_type=F32)
         + jnp.dot(yb, wo_ref[CONV_DIM:, :], preferred_element_type=F32))
    x1 = x_ref[...] + mod_ref[2:3, :] * o
    h2 = (_rms(x1, g2_ref[...]) * (1.0 + mod_ref[4:5, :]) + mod_ref[3:4, :]).astype(BF16)
    f = None
    for c0 in range(0, d_ff, FFN_CHUNK):
        u = jnp.dot(h2, w1_ref[:, c0:c0 + FFN_CHUNK], preferred_element_type=F32)
        r = jnp.maximum(u, 0.0)
        fc = jnp.dot((r * r).astype(BF16), w2_ref[c0:c0 + FFN_CHUNK, :],
                     preferred_element_type=F32)
        f = fc if f is None else f + fc
    o_ref[...] = x1 + mod_ref[5:6, :] * f


def _mix_ffn(mix_a, y_attn, x2, mod3, attn_out_g, w_out_b, norm2_g, w1_b, w2_b, seq):
    rows, d = x2.shape
    tm = ROW_TILE
    tiles_per_seq = seq // tm
    return pl.pallas_call(
        _mix_ffn_kernel,
        grid=(rows // tm,),
        in_specs=[pl.BlockSpec((tm, CONV_DIM), lambda i: (i, 0)),
                  pl.BlockSpec((tm, SB_DIM), lambda i: (i, 0)),
                  pl.BlockSpec((tm, d), lambda i: (i, 0)),
                  pl.BlockSpec((None, N_MOD, d), lambda i: (i // tiles_per_seq, 0, 0)),
                  _const_spec((1, SB_DIM)),
                  _const_spec(w_out_b.shape),
                  _const_spec((1, d)),
                  _const_spec(w1_b.shape),
                  _const_spec(w2_b.shape)],
        out_specs=pl.BlockSpec((tm, d), lambda i: (i, 0)),
        out_shape=jax.ShapeDtypeStruct((rows, d), F32),
        compiler_params=pltpu.CompilerParams(dimension_semantics=("parallel",),
                                             vmem_limit_bytes=VMEM_LIMIT),
        name="mix_ffn",
    )(mix_a, y_attn, x2, mod3, attn_out_g.reshape(1, SB_DIM), w_out_b, norm2_g.reshape(1, d),
      w1_b, w2_b)


def kernel(x, c, w_ada, b_ada, norm1_g, w_in, conv_w, q_norm_g, k_norm_g, conv_out_g,
           attn_out_g, w_out, norm2_g, w_ff1, w_ff2):
    bsz, seq, d = x.shape
    assert seq % IN_ROW_TILE == 0 and seq % ROW_TILE == 0 and seq % Q_TILE == 0
    assert Q_TILE % ATT_TILE == 0 and (seq // ATT_TILE) % PREP_TILES == 0
    assert w_in.shape[1] == 3 * CONV_DIM + 3 * SB_DIM
    x2 = x.reshape(bsz * seq, d)
    mod3 = _adaln(c, w_ada, b_ada).reshape(bsz, N_MOD, d)
    mix_a, qkv = _in_proj(x2, mod3, norm1_g, w_in.astype(BF16), conv_w, conv_out_g, seq)
    y_attn = _sb_attn(qkv, q_norm_g, k_norm_g, bsz, seq)
    out = _mix_ffn(mix_a, y_attn, x2, mod3, attn_out_g, w_out.astype(BF16), norm2_g,
                   w_ff1.astype(BF16), w_ff2.astype(BF16), seq)
    return out.reshape(bsz, seq, d)
```

```python
import functools

import jax
import jax.numpy as jnp
from jax import lax
from jax.experimental import pallas as pl
from jax.experimental.pallas import tpu as pltpu

EPS = 1e-6
CONV_DIM = 512
SB_DIM = 512
SB_HEAD_DIM = 64
N_MOD = 6
LANES = 128
ATT_TILE = 128
Q_TILE = 512
PREP_TILES = 4
LOG2E = 1.4426950408889634
ROW_TILE = 512
IN_ROW_TILE = 1024
IN_SUB_ROWS = 256
FFN_CHUNK = 512
VMEM_LIMIT = 56 * 1024 * 1024

F32 = jnp.float32
BF16 = jnp.bfloat16


def _split_bf16(a):
    hi = a.astype(BF16)
    lo = (a - hi.astype(F32)).astype(BF16)
    return hi, lo


def _const_spec(shape):
    return pl.BlockSpec(shape, lambda *_: (0,) * len(shape), pipeline_mode=pl.Buffered(1))


def _adaln_kernel(c_ref, w_ref, b_ref, o_ref):
    c = c_ref[...]
    s = c * (1.0 / (1.0 + jnp.exp(-c)))
    s_hi, s_lo = _split_bf16(s)
    w_hi, w_lo = _split_bf16(w_ref[...])
    dot = functools.partial(jnp.dot, preferred_element_type=F32)
    o_ref[...] = dot(s_hi, w_hi) + (dot(s_hi, w_lo) + dot(s_lo, w_hi)) + b_ref[...]


def _adaln(c, w_ada, b_ada):
    bsz, d = c.shape
    n = w_ada.shape[1]
    tn = 1024
    return pl.pallas_call(
        _adaln_kernel,
        grid=(n // tn,),
        in_specs=[pl.BlockSpec((bsz, d), lambda j: (0, 0)),
                  pl.BlockSpec((d, tn), lambda j: (0, j)),
                  pl.BlockSpec((1, tn), lambda j: (0, j))],
        out_specs=pl.BlockSpec((bsz, tn), lambda j: (0, j)),
        out_shape=jax.ShapeDtypeStruct((bsz, n), F32),
        compiler_params=pltpu.CompilerParams(dimension_semantics=("parallel",),
                                             vmem_limit_bytes=VMEM_LIMIT),
        name="adaln",
    )(c, w_ada, b_ada.reshape(1, n))


def _rms(xf, g):
    ms = jnp.mean(xf * xf, axis=-1, keepdims=True)
    return xf * lax.rsqrt(ms + EPS) * g


def _in_proj_kernel(tiles_per_seq, x_ref, mod_ref, g1_ref, w_ref, cw_ref, cg_ref,
                    mix_ref, qkv_ref, ext_ref, hb_ref, p_ref):
    tm = x_ref.shape[0]
    i = pl.program_id(0)

    @pl.when(i % tiles_per_seq == 0)
    def _():
        ext_ref[0:8, :] = jnp.zeros((8, CONV_DIM), F32)

    @pl.when(i % tiles_per_seq != 0)
    def _():
        ext_ref[0:8, :] = ext_ref[tm:tm + 8, :]

    shift = mod_ref[0:1, :]
    scale = mod_ref[1:2, :]
    n_conv = 3 * CONV_DIM
    n_sub = tm // IN_SUB_ROWS

    def rows(k):
        return slice(k * IN_SUB_ROWS, (k + 1) * IN_SUB_ROWS)

    def norm(k):
        h = _rms(x_ref[rows(k), :], g1_ref[...]) * (1.0 + scale) + shift
        hb_ref[rows(k), :] = h.astype(BF16)

    def project(k):
        hb = hb_ref[rows(k), :]
        p_ref[k % 2] = jnp.dot(hb, w_ref[:, :n_conv], preferred_element_type=F32)
        qkv_ref[rows(k), :] = jnp.dot(hb, w_ref[:, n_conv:], preferred_element_type=F32)

    def conv_mixer(k):
        r0 = k * IN_SUB_ROWS
        r1 = r0 + IN_SUB_ROWS
        b_gate = p_ref[k % 2, :, :CONV_DIM]
        cu = p_ref[k % 2, :, CONV_DIM:2 * CONV_DIM] * p_ref[k % 2, :, 2 * CONV_DIM:]
        ext_ref[r0 + 8:r1 + 8, :] = cu
        conv = (cw_ref[2:3, :] * cu + cw_ref[1:2, :] * ext_ref[r0 + 7:r1 + 7, :]
                + cw_ref[0:1, :] * ext_ref[r0 + 6:r1 + 6, :])
        mix_ref[rows(k), :] = _rms(b_gate * conv, cg_ref[...]).astype(BF16)

    norm(0)
    for k in range(n_sub):
        if k + 1 < n_sub:
            norm(k + 1)
        if k >= 1:
            conv_mixer(k - 1)
        project(k)
    conv_mixer(n_sub - 1)


def _in_proj(x2, mod3, norm1_g, w_in_b, conv_w, conv_out_g, seq):
    rows, d = x2.shape
    tm = IN_ROW_TILE
    tiles_per_seq = seq // tm
    n_in = w_in_b.shape[1]
    n_qkv = n_in - 3 * CONV_DIM
    return pl.pallas_call(
        functools.partial(_in_proj_kernel, tiles_per_seq),
        grid=(rows // tm,),
        in_specs=[pl.BlockSpec((tm, d), lambda i: (i, 0)),
                  pl.BlockSpec((None, N_MOD, d), lambda i: (i // tiles_per_seq, 0, 0)),
                  _const_spec((1, d)),
                  _const_spec((d, n_in)),
                  _const_spec(conv_w.shape),
                  _const_spec((1, CONV_DIM))],
        out_specs=[pl.BlockSpec((tm, CONV_DIM), lambda i: (i, 0)),
                   pl.BlockSpec((tm, n_qkv), lambda i: (i, 0))],
        out_shape=[jax.ShapeDtypeStruct((rows, CONV_DIM), BF16),
                   jax.ShapeDtypeStruct((rows, n_qkv), F32)],
        scratch_shapes=[pltpu.VMEM((tm + 8, CONV_DIM), F32),
                        pltpu.VMEM((tm, d), BF16),
                        pltpu.VMEM((2, IN_SUB_ROWS, 3 * CONV_DIM), F32)],
        compiler_params=pltpu.CompilerParams(dimension_semantics=("arbitrary",),
                                             vmem_limit_bytes=VMEM_LIMIT),
        name="in_proj",
    )(x2, mod3, norm1_g.reshape(1, d), w_in_b, conv_w, conv_out_g.reshape(1, CONV_DIM))


def _head_rms(a, g, hsum):
    s_hi, s_lo = _split_bf16(a * a)
    ssq = jnp.dot(jnp.concatenate([s_hi, s_lo], axis=1), hsum, preferred_element_type=F32)
    return a * lax.rsqrt(ssq * (1.0 / SB_HEAD_DIM) + EPS) * g


def _sb_attn_kernel(q_ref, k_ref, v_ref, qg_ref, kg_ref, tri_ref, hsum_ref, o_ref,
                    kbd_ref, vbd_ref, qn_ref, acc_ref, carry_ref,
                    z_ref, lb_ref, lst_ref, pre_ref, tot_ref):
    t = ATT_TILE
    sub = Q_TILE // t
    n_ktiles = k_ref.shape[0] // t
    n_qtiles = q_ref.shape[0] // Q_TILE
    lo_lane = lax.broadcasted_iota(jnp.int32, (t, LANES), 1) < SB_HEAD_DIM
    lo_row = lax.broadcasted_iota(jnp.int32, (LANES, t), 0) < SB_HEAD_DIM
    row = lax.broadcasted_iota(jnp.int32, (t, 2 * t), 0)
    col = lax.broadcasted_iota(jnp.int32, (t, 2 * t), 1) & (t - 1)
    causal = col < row

    def prep(c, _):
        for i in range(PREP_TILES):
            j = c * PREP_TILES + i
            r0 = pl.multiple_of(j * t, t)
            kt = _head_rms(k_ref[pl.ds(r0, t), :], kg_ref[...], hsum_ref[...]).T
            kbd_ref[j, :, :t] = jnp.where(lo_row, kt, 0.0).astype(BF16)
            kbd_ref[j, :, t:] = jnp.where(lo_row, 0.0, kt).astype(BF16)
            v = v_ref[pl.ds(r0, t), :]
            vbd_ref[j, :t, :] = jnp.where(lo_lane, v, 0.0).astype(BF16)
            vbd_ref[j, t:, :] = jnp.where(lo_lane, 0.0, v).astype(BF16)
            qn = _head_rms(q_ref[pl.ds(r0, t), :], qg_ref[...], hsum_ref[...])
            qn_ref[pl.ds(r0, t), :] = (qn * (SB_HEAD_DIM ** -0.5 * LOG2E)).astype(BF16)
        return 0

    lax.fori_loop(0, n_ktiles // PREP_TILES, prep, 0)


    def qk(step):
        j, d, _, q0, slot = step
        q = qn_ref[pl.ds(q0 + d * t, Q_TILE - d * t), :]
        z_ref[slot, d * t:, :] = jnp.dot(q, kbd_ref[j], preferred_element_type=F32)

    def scores(step):
        _, d, diag, _, slot = step
        r0 = d * t
        n = Q_TILE - r0
        z = z_ref[slot, r0:, :]
        sp = jnp.log2(1.0 + jnp.exp2(-jnp.abs(z)))
        log_beta = jnp.minimum(z, 0.0) - sp
        log_1mb = log_beta - z
        if diag:
            masked = jnp.where(causal, log_1mb[:t], 0.0)
            log_1mb = masked if n == t else jnp.concatenate([masked, log_1mb[t:]], axis=0)
        lb_ref[slot, r0:, :] = log_beta
        l16 = log_1mb.astype(BF16)
        lst_ref[slot, 0, r0:, :] = l16[:, :t]
        lst_ref[slot, 1, r0:, :] = l16[:, t:]

    def cumsum(step):
        r0 = step[1] * t
        slot = step[4]
        for h in range(2):
            c = h * t
            cum = jnp.dot(lst_ref[slot, h, r0:, :], tri_ref[...], preferred_element_type=F32)
            pre_ref[slot, r0:, c:c + t] = lb_ref[slot, r0:, c:c + t] + cum[:, :t]
            tot_ref[slot, r0:, c:c + t] = cum[:, t:]

    def weights(step):
        j, d, diag, _, slot = step
        r0 = d * t
        n = Q_TILE - r0
        pre_s, tot_s = pre_ref.at[slot], tot_ref.at[slot]
        if diag:
            a = jnp.where(causal, jnp.exp2(pre_s[r0:r0 + t, :]), 0.0)
            carry_ref[r0:r0 + t, :] = tot_s[r0:r0 + t, :]
            if n > t:
                carry = carry_ref[r0 + t:, :]
                a = jnp.concatenate([a, jnp.exp2(pre_s[r0 + t:, :] + carry)], axis=0)
                carry_ref[r0 + t:, :] = carry + tot_s[r0 + t:, :]
        else:
            carry = carry_ref[...]
            a = jnp.exp2(pre_s[...] + carry)
            carry_ref[...] = carry + tot_s[...]
        av = jnp.dot(a.astype(BF16), vbd_ref[j], preferred_element_type=F32)
        if diag:
            acc_ref[r0:r0 + t, :] = av[:t]
            if n > t:
                acc_ref[r0 + t:, :] += av[t:]
        else:
            acc_ref[...] += av

    stages = (qk, scores, cumsum, weights)
    n_stages = len(stages)

    steps = []
    for qi in range(n_qtiles):
        jd = qi * sub
        tile_steps = [(jd + d, d, True, qi * Q_TILE) for d in reversed(range(sub))]
        tile_steps += [(j, 0, False, qi * Q_TILE) for j in reversed(range(jd))]
        steps += [(st + ((len(steps) + i) % 2,), i == len(tile_steps) - 1)
                  for i, st in enumerate(tile_steps)]
    for g in range(len(steps) + n_stages - 1):
        for s in reversed(range(n_stages)):
            if 0 <= g - s < len(steps):
                st, last_of_tile = steps[g - s]
                stages[s](st)
                if last_of_tile and s == n_stages - 1:
                    o_ref[st[3]:st[3] + Q_TILE, :] = acc_ref[...]


def _sb_attn(qkv, q_norm_g, k_norm_g, bsz, seq):
    rows = qkv.shape[0]
    t = ATT_TILE
    n_pairs = SB_DIM // LANES
    n_tiles = seq // t
    j_idx = jnp.arange(t)[:, None]
    s_idx = jnp.arange(2 * t)[None, :]
    tri = ((s_idx >= t) | (j_idx > s_idx)).astype(BF16)
    lane_head = jnp.arange(LANES) // SB_HEAD_DIM
    hsum = (jnp.tile(lane_head, 2)[:, None] == lane_head[None, :]).astype(BF16)
    reps = LANES // SB_HEAD_DIM
    qg = jnp.tile(q_norm_g, reps).reshape(1, LANES)
    kg = jnp.tile(k_norm_g, reps).reshape(1, LANES)
    return pl.pallas_call(
        _sb_attn_kernel,
        grid=(bsz, n_pairs),
        in_specs=[pl.BlockSpec((seq, LANES), lambda b, p: (b, p)),
                  pl.BlockSpec((seq, LANES), lambda b, p: (b, n_pairs + p)),
                  pl.BlockSpec((seq, LANES), lambda b, p: (b, 2 * n_pairs + p)),
                  _const_spec((1, LANES)),
                  _const_spec((1, LANES)),
                  _const_spec((t, 2 * t)),
                  _const_spec((2 * LANES, LANES))],
        out_specs=pl.BlockSpec((seq, LANES), lambda b, p: (b, p)),
        out_shape=jax.ShapeDtypeStruct((rows, SB_DIM), F32),
        scratch_shapes=[pltpu.VMEM((n_tiles, LANES, 2 * t), BF16),
                        pltpu.VMEM((n_tiles, 2 * t, LANES), BF16),
                        pltpu.VMEM((seq, LANES), BF16),
                        pltpu.VMEM((Q_TILE, LANES), F32),
                        pltpu.VMEM((Q_TILE, 2 * t), F32),
                        pltpu.VMEM((2, Q_TILE, 2 * t), F32),
                        pltpu.VMEM((2, Q_TILE, 2 * t), F32),
                        pltpu.VMEM((2, 2, Q_TILE, t), BF16),
                        pltpu.VMEM((2, Q_TILE, 2 * t), F32),
                        pltpu.VMEM((2, Q_TILE, 2 * t), F32)],
        compiler_params=pltpu.CompilerParams(dimension_semantics=("parallel", "parallel"),
                                             vmem_limit_bytes=VMEM_LIMIT),
        name="sb_attn",
    )(qkv, qkv, qkv, qg, kg, tri, hsum)


def _mix_ffn_kernel(mix_ref, ya_ref, x_ref, mod_ref, ag_ref, wo_ref, g2_ref, w1_ref, w2_ref,
                    o_ref):
    d_ff = w1_ref.shape[1]
    yb = _rms(ya_ref[...], ag_ref[...]).astype(BF16)
    o = (jnp.dot(mix_ref[...], wo_ref[:CONV_DIM, :], preferred_element_type=F32)
         + jnp.dot(yb, wo_ref[CONV_DIM:, :], preferred_element_type=F32))
    x1 = x_ref[...] + mod_ref[2:3, :] * o
    h2 = (_rms(x1, g2_ref[...]) * (1.0 + mod_ref[4:5, :]) + mod_ref[3:4, :]).astype(BF16)
    f = None
    for c0 in range(0, d_ff, FFN_CHUNK):
        u = jnp.dot(h2, w1_ref[:, c0:c0 + FFN_CHUNK], preferred_element_type=F32)
        r = jnp.maximum(u, 0.0)
        fc = jnp.dot((r * r).astype(BF16), w2_ref[c0:c0 + FFN_CHUNK, :],
                     preferred_element_type=F32)
        f = fc if f is None else f + fc
    o_ref[...] = x1 + mod_ref[5:6, :] * f


def _mix_ffn(mix_a, y_attn, x2, mod3, attn_out_g, w_out_b, norm2_g, w1_b, w2_b, seq):
    rows, d = x2.shape
    tm = ROW_TILE
    tiles_per_seq = seq // tm
    return pl.pallas_call(
        _mix_ffn_kernel,
        grid=(rows // tm,),
        in_specs=[pl.BlockSpec((tm, CONV_DIM), lambda i: (i, 0)),
                  pl.BlockSpec((tm, SB_DIM), lambda i: (i, 0)),
                  pl.BlockSpec((tm, d), lambda i: (i, 0)),
                  pl.BlockSpec((None, N_MOD, d), lambda i: (i // tiles_per_seq, 0, 0)),
                  _const_spec((1, SB_DIM)),
                  _const_spec(w_out_b.shape),
                  _const_spec((1, d)),
                  _const_spec(w1_b.shape),
                  _const_spec(w2_b.shape)],
        out_specs=pl.BlockSpec((tm, d), lambda i: (i, 0)),
        out_shape=jax.ShapeDtypeStruct((rows, d), F32),
        compiler_params=pltpu.CompilerParams(dimension_semantics=("parallel",),
                                             vmem_limit_bytes=VMEM_LIMIT),
        name="mix_ffn",
    )(mix_a, y_attn, x2, mod3, attn_out_g.reshape(1, SB_DIM), w_out_b, norm2_g.reshape(1, d),
      w1_b, w2_b)


def kernel(x, c, w_ada, b_ada, norm1_g, w_in, conv_w, q_norm_g, k_norm_g, conv_out_g,
           attn_out_g, w_out, norm2_g, w_ff1, w_ff2):
    bsz, seq, d = x.shape
    assert seq % IN_ROW_TILE == 0 and seq % ROW_TILE == 0 and seq % Q_TILE == 0
    assert Q_TILE % ATT_TILE == 0 and (seq // ATT_TILE) % PREP_TILES == 0
    assert w_in.shape[1] == 3 * CONV_DIM + 3 * SB_DIM
    x2 = x.reshape(bsz * seq, d)
    mod3 = _adaln(c, w_ada, b_ada).reshape(bsz, N_MOD, d)
    mix_a, qkv = _in_proj(x2, mod3, norm1_g, w_in.astype(BF16), conv_w, conv_out_g, seq)
    y_attn = _sb_attn(qkv, q_norm_g, k_norm_g, bsz, seq)
    out = _mix_ffn(mix_a, y_attn, x2, mod3, attn_out_g, w_out.astype(BF16), norm2_g,
                   w_ff1.astype(BF16), w_ff2.astype(BF16), seq)
    return out.reshape(bsz, seq, d)
```

```python
import functools

import jax
import jax.numpy as jnp
from jax import lax
from jax.experimental import pallas as pl
from jax.experimental.pallas import tpu as pltpu

EPS = 1e-6
CONV_DIM = 512
SB_DIM = 512
SB_HEAD_DIM = 64
N_MOD = 6
LANES = 128
ATT_TILE = 128
Q_TILE = 512
PREP_TILES = 4
LOG2E = 1.4426950408889634
ROW_TILE = 512
IN_ROW_TILE = 1024
IN_SUB_ROWS = 512
FFN_CHUNK = 512
VMEM_LIMIT = 56 * 1024 * 1024

F32 = jnp.float32
BF16 = jnp.bfloat16


def _split_bf16(a):
    hi = a.astype(BF16)
    lo = (a - hi.astype(F32)).astype(BF16)
    return hi, lo


def _const_spec(shape):
    return pl.BlockSpec(shape, lambda *_: (0,) * len(shape), pipeline_mode=pl.Buffered(1))


def _adaln_kernel(c_ref, w_ref, b_ref, o_ref):
    c = c_ref[...]
    s = c * (1.0 / (1.0 + jnp.exp(-c)))
    s_hi, s_lo = _split_bf16(s)
    w_hi, w_lo = _split_bf16(w_ref[...])
    dot = functools.partial(jnp.dot, preferred_element_type=F32)
    o_ref[...] = dot(s_hi, w_hi) + (dot(s_hi, w_lo) + dot(s_lo, w_hi)) + b_ref[...]


def _adaln(c, w_ada, b_ada):
    bsz, d = c.shape
    n = w_ada.shape[1]
    tn = 1024
    return pl.pallas_call(
        _adaln_kernel,
        grid=(n // tn,),
        in_specs=[pl.BlockSpec((bsz, d), lambda j: (0, 0)),
                  pl.BlockSpec((d, tn), lambda j: (0, j)),
                  pl.BlockSpec((1, tn), lambda j: (0, j))],
        out_specs=pl.BlockSpec((bsz, tn), lambda j: (0, j)),
        out_shape=jax.ShapeDtypeStruct((bsz, n), F32),
        compiler_params=pltpu.CompilerParams(dimension_semantics=("parallel",),
                                             vmem_limit_bytes=VMEM_LIMIT),
        name="adaln",
    )(c, w_ada, b_ada.reshape(1, n))


def _rms(xf, g):
    ms = jnp.mean(xf * xf, axis=-1, keepdims=True)
    return xf * lax.rsqrt(ms + EPS) * g


def _in_proj_kernel(tiles_per_seq, x_ref, mod_ref, g1_ref, w_ref, cw_ref, cg_ref,
                    mix_ref, qkv_ref, ext_ref, hb_ref, p_ref):
    tm = x_ref.shape[0]
    i = pl.program_id(0)

    @pl.when(i % tiles_per_seq == 0)
    def _():
        ext_ref[0:8, :] = jnp.zeros((8, CONV_DIM), F32)

    @pl.when(i % tiles_per_seq != 0)
    def _():
        ext_ref[0:8, :] = ext_ref[tm:tm + 8, :]

    shift = mod_ref[0:1, :]
    scale = mod_ref[1:2, :]
    n_conv = 3 * CONV_DIM
    n_sub = tm // IN_SUB_ROWS

    def rows(k):
        return slice(k * IN_SUB_ROWS, (k + 1) * IN_SUB_ROWS)

    def norm(k):
        h = _rms(x_ref[rows(k), :], g1_ref[...]) * (1.0 + scale) + shift
        hb_ref[rows(k), :] = h.astype(BF16)

    def project(k):
        hb = hb_ref[rows(k), :]
        p_ref[k % 2] = jnp.dot(hb, w_ref[:, :n_conv], preferred_element_type=F32)
        qkv_ref[rows(k), :] = jnp.dot(hb, w_ref[:, n_conv:], preferred_element_type=F32)

    def conv_mixer(k):
        r0 = k * IN_SUB_ROWS
        r1 = r0 + IN_SUB_ROWS
        b_gate = p_ref[k % 2, :, :CONV_DIM]
        cu = p_ref[k % 2, :, CONV_DIM:2 * CONV_DIM] * p_ref[k % 2, :, 2 * CONV_DIM:]
        ext_ref[r0 + 8:r1 + 8, :] = cu
        conv = (cw_ref[2:3, :] * cu + cw_ref[1:2, :] * ext_ref[r0 + 7:r1 + 7, :]
                + cw_ref[0:1, :] * ext_ref[r0 + 6:r1 + 6, :])
        mix_ref[rows(k), :] = _rms(b_gate * conv, cg_ref[...]).astype(BF16)

    norm(0)
    for k in range(n_sub):
        if k + 1 < n_sub:
            norm(k + 1)
        if k >= 1:
            conv_mixer(k - 1)
        project(k)
    conv_mixer(n_sub - 1)


def _in_proj(x2, mod3, norm1_g, w_in_b, conv_w, conv_out_g, seq):
    rows, d = x2.shape
    tm = IN_ROW_TILE
    tiles_per_seq = seq // tm
    n_in = w_in_b.shape[1]
    n_qkv = n_in - 3 * CONV_DIM
    return pl.pallas_call(
        functools.partial(_in_proj_kernel, tiles_per_seq),
        grid=(rows // tm,),
        in_specs=[pl.BlockSpec((tm, d), lambda i: (i, 0)),
                  pl.BlockSpec((None, N_MOD, d), lambda i: (i // tiles_per_seq, 0, 0)),
                  _const_spec((1, d)),
                  _const_spec((d, n_in)),
                  _const_spec(conv_w.shape),
                  _const_spec((1, CONV_DIM))],
        out_specs=[pl.BlockSpec((tm, CONV_DIM), lambda i: (i, 0)),
                   pl.BlockSpec((tm, n_qkv), lambda i: (i, 0))],
        out_shape=[jax.ShapeDtypeStruct((rows, CONV_DIM), BF16),
                   jax.ShapeDtypeStruct((rows, n_qkv), F32)],
        scratch_shapes=[pltpu.VMEM((tm + 8, CONV_DIM), F32),
                        pltpu.VMEM((tm, d), BF16),
                        pltpu.VMEM((2, IN_SUB_ROWS, 3 * CONV_DIM), F32)],
        compiler_params=pltpu.CompilerParams(dimension_semantics=("arbitrary",),
                                             vmem_limit_bytes=VMEM_LIMIT),
        name="in_proj",
    )(x2, mod3, norm1_g.reshape(1, d), w_in_b, conv_w, conv_out_g.reshape(1, CONV_DIM))


def _head_rms(a, g, hsum):
    s_hi, s_lo = _split_bf16(a * a)
    ssq = jnp.dot(jnp.concatenate([s_hi, s_lo], axis=1), hsum, preferred_element_type=F32)
    return a * lax.rsqrt(ssq * (1.0 / SB_HEAD_DIM) + EPS) * g


def _sb_attn_kernel(n_cast, q_ref, k_ref, v_ref, qg_ref, kg_ref, tri_ref, hsum_ref, *refs):
    w32_refs, o_ref, w16_refs = refs[:n_cast], refs[n_cast], refs[n_cast + 1:2 * n_cast + 1]
    (kbd_ref, vbd_ref, qn_ref, acc_ref, carry_ref,
     z_ref, lb_ref, lst_ref, pre_ref, tot_ref) = refs[2 * n_cast + 1:]
    for w32_ref, w16_ref in zip(w32_refs, w16_refs):
        w16_ref[...] = w32_ref[...].astype(BF16)

    t = ATT_TILE
    sub = Q_TILE // t
    n_ktiles = k_ref.shape[0] // t
    n_qtiles = q_ref.shape[0] // Q_TILE
    lo_lane = lax.broadcasted_iota(jnp.int32, (t, LANES), 1) < SB_HEAD_DIM
    lo_row = lax.broadcasted_iota(jnp.int32, (LANES, t), 0) < SB_HEAD_DIM
    row = lax.broadcasted_iota(jnp.int32, (t, 2 * t), 0)
    col = lax.broadcasted_iota(jnp.int32, (t, 2 * t), 1) & (t - 1)
    causal = col < row

    def prep(c, _):
        for i in range(PREP_TILES):
            j = c * PREP_TILES + i
            r0 = pl.multiple_of(j * t, t)
            kt = _head_rms(k_ref[pl.ds(r0, t), :], kg_ref[...], hsum_ref[...]).T
            kbd_ref[j, :, :t] = jnp.where(lo_row, kt, 0.0).astype(BF16)
            kbd_ref[j, :, t:] = jnp.where(lo_row, 0.0, kt).astype(BF16)
            v = v_ref[pl.ds(r0, t), :]
            vbd_ref[j, :t, :] = jnp.where(lo_lane, v, 0.0).astype(BF16)
            vbd_ref[j, t:, :] = jnp.where(lo_lane, 0.0, v).astype(BF16)
            qn = _head_rms(q_ref[pl.ds(r0, t), :], qg_ref[...], hsum_ref[...])
            qn_ref[pl.ds(r0, t), :] = (qn * (SB_HEAD_DIM ** -0.5 * LOG2E)).astype(BF16)
        return 0

    lax.fori_loop(0, n_ktiles // PREP_TILES, prep, 0)


    def qk(step):
        j, d, _, q0, slot = step
        q = qn_ref[pl.ds(q0 + d * t, Q_TILE - d * t), :]
        z_ref[slot, d * t:, :] = jnp.dot(q, kbd_ref[j], preferred_element_type=F32)

    def scores(step):
        _, d, diag, _, slot = step
        r0 = d * t
        n = Q_TILE - r0
        z = z_ref[slot, r0:, :]
        sp = jnp.log2(1.0 + jnp.exp2(-jnp.abs(z)))
        log_beta = jnp.minimum(z, 0.0) - sp
        log_1mb = log_beta - z
        if diag:
            masked = jnp.where(causal, log_1mb[:t], 0.0)
            log_1mb = masked if n == t else jnp.concatenate([masked, log_1mb[t:]], axis=0)
        lb_ref[slot, r0:, :] = log_beta
        l16 = log_1mb.astype(BF16)
        lst_ref[slot, 0, r0:, :] = l16[:, :t]
        lst_ref[slot, 1, r0:, :] = l16[:, t:]

    def cumsum(step):
        r0 = step[1] * t
        slot = step[4]
        for h in range(2):
            c = h * t
            cum = jnp.dot(lst_ref[slot, h, r0:, :], tri_ref[...], preferred_element_type=F32)
            pre_ref[slot, r0:, c:c + t] = lb_ref[slot, r0:, c:c + t] + cum[:, :t]
            tot_ref[slot, r0:, c:c + t] = cum[:, t:]

    def weights(step):
        j, d, diag, _, slot = step
        r0 = d * t
        n = Q_TILE - r0
        pre_s, tot_s = pre_ref.at[slot], tot_ref.at[slot]
        if diag:
            a = jnp.where(causal, jnp.exp2(pre_s[r0:r0 + t, :]), 0.0)
            carry_ref[r0:r0 + t, :] = tot_s[r0:r0 + t, :]
            if n > t:
                carry = carry_ref[r0 + t:, :]
                a = jnp.concatenate([a, jnp.exp2(pre_s[r0 + t:, :] + carry)], axis=0)
                carry_ref[r0 + t:, :] = carry + tot_s[r0 + t:, :]
        else:
            carry = carry_ref[...]
            a = jnp.exp2(pre_s[...] + carry)
            carry_ref[...] = carry + tot_s[...]
        av = jnp.dot(a.astype(BF16), vbd_ref[j], preferred_element_type=F32)
        if diag:
            acc_ref[r0:r0 + t, :] = av[:t]
            if n > t:
                acc_ref[r0 + t:, :] += av[t:]
        else:
            acc_ref[...] += av

    stages = (qk, scores, cumsum, weights)
    n_stages = len(stages)

    steps = []
    for qi in range(n_qtiles):
        jd = qi * sub
        tile_steps = [(jd + d, d, True, qi * Q_TILE) for d in reversed(range(sub))]
        tile_steps += [(j, 0, False, qi * Q_TILE) for j in reversed(range(jd))]
        steps += [(st + ((len(steps) + i) % 2,), i == len(tile_steps) - 1)
                  for i, st in enumerate(tile_steps)]
    for g in range(len(steps) + n_stages - 1):
        for s in reversed(range(n_stages)):
            if 0 <= g - s < len(steps):
                st, last_of_tile = steps[g - s]
                stages[s](st)
                if last_of_tile and s == n_stages - 1:
                    o_ref[st[3]:st[3] + Q_TILE, :] = acc_ref[...]


def _sb_attn(qkv, q_norm_g, k_norm_g, bsz, seq, cast_weights):
    rows = qkv.shape[0]
    t = ATT_TILE
    n_pairs = SB_DIM // LANES
    n_tiles = seq // t
    n_steps = bsz * n_pairs
    slabs = [w.shape[0] // n_steps for w in cast_weights]
    assert all(w.shape[0] == s * n_steps and s % 16 == 0 for w, s in zip(cast_weights, slabs))
    slab_specs = [pl.BlockSpec((s, w.shape[1]), lambda b, p: (b * n_pairs + p, 0))
                  for w, s in zip(cast_weights, slabs)]
    j_idx = jnp.arange(t)[:, None]
    s_idx = jnp.arange(2 * t)[None, :]
    tri = ((s_idx >= t) | (j_idx > s_idx)).astype(BF16)
    lane_head = jnp.arange(LANES) // SB_HEAD_DIM
    hsum = (jnp.tile(lane_head, 2)[:, None] == lane_head[None, :]).astype(BF16)
    reps = LANES // SB_HEAD_DIM
    qg = jnp.tile(q_norm_g, reps).reshape(1, LANES)
    kg = jnp.tile(k_norm_g, reps).reshape(1, LANES)
    return pl.pallas_call(
        functools.partial(_sb_attn_kernel, len(cast_weights)),
        grid=(bsz, n_pairs),
        in_specs=[pl.BlockSpec((seq, LANES), lambda b, p: (b, p)),
                  pl.BlockSpec((seq, LANES), lambda b, p: (b, n_pairs + p)),
                  pl.BlockSpec((seq, LANES), lambda b, p: (b, 2 * n_pairs + p)),
                  _const_spec((1, LANES)),
                  _const_spec((1, LANES)),
                  _const_spec((t, 2 * t)),
                  _const_spec((2 * LANES, LANES))] + slab_specs,
        out_specs=[pl.BlockSpec((seq, LANES), lambda b, p: (b, p))] + slab_specs,
        out_shape=[jax.ShapeDtypeStruct((rows, SB_DIM), F32)]
        + [jax.ShapeDtypeStruct(w.shape, BF16) for w in cast_weights],
        scratch_shapes=[pltpu.VMEM((n_tiles, LANES, 2 * t), BF16),
                        pltpu.VMEM((n_tiles, 2 * t, LANES), BF16),
                        pltpu.VMEM((seq, LANES), BF16),
                        pltpu.VMEM((Q_TILE, LANES), F32),
                        pltpu.VMEM((Q_TILE, 2 * t), F32),
                        pltpu.VMEM((2, Q_TILE, 2 * t), F32),
                        pltpu.VMEM((2, Q_TILE, 2 * t), F32),
                        pltpu.VMEM((2, 2, Q_TILE, t), BF16),
                        pltpu.VMEM((2, Q_TILE, 2 * t), F32),
                        pltpu.VMEM((2, Q_TILE, 2 * t), F32)],
        compiler_params=pltpu.CompilerParams(dimension_semantics=("parallel", "parallel"),
                                             vmem_limit_bytes=VMEM_LIMIT),
        name="sb_attn",
    )(qkv, qkv, qkv, qg, kg, tri, hsum, *cast_weights)


def _mix_ffn_kernel(mix_ref, ya_ref, x_ref, mod_ref, ag_ref, wo_ref, g2_ref, w1_ref, w2_ref,
                    o_ref):
    d_ff = w1_ref.shape[1]
    yb = _rms(ya_ref[...], ag_ref[...]).astype(BF16)
    o = (jnp.dot(mix_ref[...], wo_ref[:CONV_DIM, :], preferred_element_type=F32)
         + jnp.dot(yb, wo_ref[CONV_DIM:, :], preferred_element_type=F32))
    x1 = x_ref[...] + mod_ref[2:3, :] * o
    h2 = (_rms(x1, g2_ref[...]) * (1.0 + mod_ref[4:5, :]) + mod_ref[3:4, :]).astype(BF16)
    f = None
    for c0 in range(0, d_ff, FFN_CHUNK):
        u = jnp.dot(h2, w1_ref[:, c0:c0 + FFN_CHUNK], preferred_element_type=F32)
        r = jnp.maximum(u, 0.0)
        fc = jnp.dot((r * r).astype(BF16), w2_ref[c0:c0 + FFN_CHUNK, :],
                     preferred_element_type=F32)
        f = fc if f is None else f + fc
    o_ref[...] = x1 + mod_ref[5:6, :] * f


def _mix_ffn(mix_a, y_attn, x2, mod3, attn_out_g, w_out_b, norm2_g, w1_b, w2_b, seq):
    rows, d = x2.shape
    tm = ROW_TILE
    tiles_per_seq = seq // tm
    return pl.pallas_call(
        _mix_ffn_kernel,
        grid=(rows // tm,),
        in_specs=[pl.BlockSpec((tm, CONV_DIM), lambda i: (i, 0)),
                  pl.BlockSpec((tm, SB_DIM), lambda i: (i, 0)),
                  pl.BlockSpec((tm, d), lambda i: (i, 0)),
                  pl.BlockSpec((None, N_MOD, d), lambda i: (i // tiles_per_seq, 0, 0)),
                  _const_spec((1, SB_DIM)),
                  _const_spec(w_out_b.shape),
                  _const_spec((1, d)),
                  _const_spec(w1_b.shape),
                  _const_spec(w2_b.shape)],
        out_specs=pl.BlockSpec((tm, d), lambda i: (i, 0)),
        out_shape=jax.ShapeDtypeStruct((rows, d), F32),
        compiler_params=pltpu.CompilerParams(dimension_semantics=("parallel",),
                                             vmem_limit_bytes=VMEM_LIMIT),
        name="mix_ffn",
    )(mix_a, y_attn, x2, mod3, attn_out_g.reshape(1, SB_DIM), w_out_b, norm2_g.reshape(1, d),
      w1_b, w2_b)


def kernel(x, c, w_ada, b_ada, norm1_g, w_in, conv_w, q_norm_g, k_norm_g, conv_out_g,
           attn_out_g, w_out, norm2_g, w_ff1, w_ff2):
    bsz, seq, d = x.shape
    assert seq % IN_ROW_TILE == 0 and seq % ROW_TILE == 0 and seq % Q_TILE == 0
    assert Q_TILE % ATT_TILE == 0 and (seq // ATT_TILE) % PREP_TILES == 0
    assert w_in.shape[1] == 3 * CONV_DIM + 3 * SB_DIM
    x2 = x.reshape(bsz * seq, d)
    mod3 = _adaln(c, w_ada, b_ada).reshape(bsz, N_MOD, d)
    mix_a, qkv = _in_proj(x2, mod3, norm1_g, w_in.astype(BF16), conv_w, conv_out_g, seq)
    y_attn, w_out_b, w1_b, w2_b = _sb_attn(qkv, q_norm_g, k_norm_g, bsz, seq,
                                           (w_out, w_ff1, w_ff2))
    out = _mix_ffn(mix_a, y_attn, x2, mod3, attn_out_g, w_out_b, norm2_g, w1_b, w2_b, seq)
    return out.reshape(bsz, seq, d)
```

```python
import functools

import jax
import jax.numpy as jnp
from jax import lax
from jax.experimental import pallas as pl
from jax.experimental.pallas import tpu as pltpu

EPS = 1e-6
CONV_DIM = 512
SB_DIM = 512
SB_HEAD_DIM = 64
N_MOD = 6
LANES = 128
ATT_TILE = 128
Q_TILE = 512
PREP_TILES = 4
LOG2E = 1.4426950408889634
ROW_TILE = 512
IN_ROW_TILE = 1024
IN_SUB_ROWS = 256
FFN_CHUNK = 512
VMEM_LIMIT = 56 * 1024 * 1024

F32 = jnp.float32
BF16 = jnp.bfloat16


def _split_bf16(a):
    hi = a.astype(BF16)
    lo = (a - hi.astype(F32)).astype(BF16)
    return hi, lo


def _const_spec(shape):
    return pl.BlockSpec(shape, lambda *_: (0,) * len(shape), pipeline_mode=pl.Buffered(1))


def _adaln_kernel(c_ref, w_ref, b_ref, o_ref):
    c = c_ref[...]
    s = c * (1.0 / (1.0 + jnp.exp(-c)))
    s_hi, s_lo = _split_bf16(s)
    w_hi, w_lo = _split_bf16(w_ref[...])
    bsz = c.shape[0]
    dot = functools.partial(jnp.dot, preferred_element_type=F32)
    both = dot(jnp.concatenate([s_hi, s_lo], axis=0), w_hi)
    o_ref[...] = both[:bsz] + (dot(s_hi, w_lo) + both[bsz:]) + b_ref[...]


def _adaln(c, w_ada, b_ada):
    bsz, d = c.shape
    n = w_ada.shape[1]
    tn = 1024
    return pl.pallas_call(
        _adaln_kernel,
        grid=(n // tn,),
        in_specs=[pl.BlockSpec((bsz, d), lambda j: (0, 0)),
                  pl.BlockSpec((d, tn), lambda j: (0, j)),
                  pl.BlockSpec((1, tn), lambda j: (0, j))],
        out_specs=pl.BlockSpec((bsz, tn), lambda j: (0, j)),
        out_shape=jax.ShapeDtypeStruct((bsz, n), F32),
        compiler_params=pltpu.CompilerParams(dimension_semantics=("parallel",),
                                             vmem_limit_bytes=VMEM_LIMIT),
        name="adaln",
    )(c, w_ada, b_ada.reshape(1, n))


def _rms(xf, g):
    ms = jnp.mean(xf * xf, axis=-1, keepdims=True)
    return xf * lax.rsqrt(ms + EPS) * g


def _in_proj_kernel(tiles_per_seq, x_ref, mod_ref, g1_ref, w32_ref, cw_ref, cg_ref,
                    mix_ref, qkv_ref, ext_ref, hb_ref, p_ref, w_ref):
    tm = x_ref.shape[0]
    i = pl.program_id(0)

    @pl.when(i == 0)
    def _():
        w_ref[...] = w32_ref[...].astype(BF16)

    @pl.when(i % tiles_per_seq == 0)
    def _():
        ext_ref[0:8, :] = jnp.zeros((8, CONV_DIM), F32)

    @pl.when(i % tiles_per_seq != 0)
    def _():
        ext_ref[0:8, :] = ext_ref[tm:tm + 8, :]

    shift = mod_ref[0:1, :]
    scale = mod_ref[1:2, :]
    n_conv = 3 * CONV_DIM
    n_sub = tm // IN_SUB_ROWS

    def rows(k):
        return slice(k * IN_SUB_ROWS, (k + 1) * IN_SUB_ROWS)

    def norm(k):
        h = _rms(x_ref[rows(k), :], g1_ref[...]) * (1.0 + scale) + shift
        hb_ref[rows(k), :] = h.astype(BF16)

    def project(k):
        hb = hb_ref[rows(k), :]
        p_ref[k % 2] = jnp.dot(hb, w_ref[:, :n_conv], preferred_element_type=F32)
        qkv_ref[rows(k), :] = jnp.dot(hb, w_ref[:, n_conv:], preferred_element_type=F32)

    def conv_mixer(k):
        r0 = k * IN_SUB_ROWS
        r1 = r0 + IN_SUB_ROWS
        b_gate = p_ref[k % 2, :, :CONV_DIM]
        cu = p_ref[k % 2, :, CONV_DIM:2 * CONV_DIM] * p_ref[k % 2, :, 2 * CONV_DIM:]
        ext_ref[r0 + 8:r1 + 8, :] = cu
        conv = (cw_ref[2:3, :] * cu + cw_ref[1:2, :] * ext_ref[r0 + 7:r1 + 7, :]
                + cw_ref[0:1, :] * ext_ref[r0 + 6:r1 + 6, :])
        mix_ref[rows(k), :] = _rms(b_gate * conv, cg_ref[...]).astype(BF16)

    norm(0)
    for k in range(n_sub):
        if k + 1 < n_sub:
            norm(k + 1)
        if k >= 1:
            conv_mixer(k - 1)
        project(k)
    conv_mixer(n_sub - 1)


def _in_proj(x2, mod3, norm1_g, w_in, conv_w, conv_out_g, seq):
    rows, d = x2.shape
    tm = IN_ROW_TILE
    tiles_per_seq = seq // tm
    n_in = w_in.shape[1]
    n_qkv = n_in - 3 * CONV_DIM
    return pl.pallas_call(
        functools.partial(_in_proj_kernel, tiles_per_seq),
        grid=(rows // tm,),
        in_specs=[pl.BlockSpec((tm, d), lambda i: (i, 0)),
                  pl.BlockSpec((None, N_MOD, d), lambda i: (i // tiles_per_seq, 0, 0)),
                  _const_spec((1, d)),
                  _const_spec((d, n_in)),
                  _const_spec(conv_w.shape),
                  _const_spec((1, CONV_DIM))],
        out_specs=[pl.BlockSpec((tm, CONV_DIM), lambda i: (i, 0)),
                   pl.BlockSpec((tm, n_qkv), lambda i: (i, 0))],
        out_shape=[jax.ShapeDtypeStruct((rows, CONV_DIM), BF16),
                   jax.ShapeDtypeStruct((rows, n_qkv), F32)],
        scratch_shapes=[pltpu.VMEM((tm + 8, CONV_DIM), F32),
                        pltpu.VMEM((tm, d), BF16),
                        pltpu.VMEM((2, IN_SUB_ROWS, 3 * CONV_DIM), F32),
                        pltpu.VMEM((d, n_in), BF16)],
        compiler_params=pltpu.CompilerParams(dimension_semantics=("arbitrary",),
                                             vmem_limit_bytes=VMEM_LIMIT),
        name="in_proj",
    )(x2, mod3, norm1_g.reshape(1, d), w_in, conv_w, conv_out_g.reshape(1, CONV_DIM))


def _head_rms(a, g, hsum):
    s_hi, s_lo = _split_bf16(a * a)
    ssq = jnp.dot(jnp.concatenate([s_hi, s_lo], axis=1), hsum, preferred_element_type=F32)
    return a * lax.rsqrt(ssq * (1.0 / SB_HEAD_DIM) + EPS) * g


def _sb_attn_kernel(n_cast, q_ref, k_ref, v_ref, qg_ref, kg_ref, tri_ref, hsum_ref, *refs):
    w32_refs, o_ref, w16_refs = refs[:n_cast], refs[n_cast], refs[n_cast + 1:2 * n_cast + 1]
    (kbd_ref, vbd_ref, qn_ref, acc_ref, carry_ref,
     z_ref, lb_ref, lst_ref, pre_ref, tot_ref) = refs[2 * n_cast + 1:]
    for w32_ref, w16_ref in zip(w32_refs, w16_refs):
        w16_ref[...] = w32_ref[...].astype(BF16)

    t = ATT_TILE
    sub = Q_TILE // t
    n_ktiles = k_ref.shape[0] // t
    n_qtiles = q_ref.shape[0] // Q_TILE
    lo_lane = lax.broadcasted_iota(jnp.int32, (t, LANES), 1) < SB_HEAD_DIM
    lo_row = lax.broadcasted_iota(jnp.int32, (LANES, t), 0) < SB_HEAD_DIM
    row = lax.broadcasted_iota(jnp.int32, (t, 2 * t), 0)
    col = lax.broadcasted_iota(jnp.int32, (t, 2 * t), 1) & (t - 1)
    causal = col < row

    def prep(c, _):
        for i in range(PREP_TILES):
            j = c * PREP_TILES + i
            r0 = pl.multiple_of(j * t, t)
            kt = _head_rms(k_ref[pl.ds(r0, t), :], kg_ref[...], hsum_ref[...]).T
            kbd_ref[j, :, :t] = jnp.where(lo_row, kt, 0.0).astype(BF16)
            kbd_ref[j, :, t:] = jnp.where(lo_row, 0.0, kt).astype(BF16)
            v = v_ref[pl.ds(r0, t), :]
            vbd_ref[j, :t, :] = jnp.where(lo_lane, v, 0.0).astype(BF16)
            vbd_ref[j, t:, :] = jnp.where(lo_lane, 0.0, v).astype(BF16)
            qn = _head_rms(q_ref[pl.ds(r0, t), :], qg_ref[...], hsum_ref[...])
            qn_ref[pl.ds(r0, t), :] = (qn * (SB_HEAD_DIM ** -0.5 * LOG2E)).astype(BF16)
        return 0

    lax.fori_loop(0, n_ktiles // PREP_TILES, prep, 0)


    def qk(step):
        j, d, _, q0, slot = step
        q = qn_ref[pl.ds(q0 + d * t, Q_TILE - d * t), :]
        z_ref[slot, d * t:, :] = jnp.dot(q, kbd_ref[j], preferred_element_type=F32)

    def scores(step):
        _, d, diag, _, slot = step
        r0 = d * t
        n = Q_TILE - r0
        z = z_ref[slot, r0:, :]
        sp = jnp.log2(1.0 + jnp.exp2(-jnp.abs(z)))
        log_beta = jnp.minimum(z, 0.0) - sp
        log_1mb = log_beta - z
        if diag:
            masked = jnp.where(causal, log_1mb[:t], 0.0)
            log_1mb = masked if n == t else jnp.concatenate([masked, log_1mb[t:]], axis=0)
        lb_ref[slot, r0:, :] = log_beta
        l16 = log_1mb.astype(BF16)
        lst_ref[slot, 0, r0:, :] = l16[:, :t]
        lst_ref[slot, 1, r0:, :] = l16[:, t:]

    def cumsum(step):
        r0 = step[1] * t
        slot = step[4]
        for h in range(2):
            c = h * t
            cum = jnp.dot(lst_ref[slot, h, r0:, :], tri_ref[...], preferred_element_type=F32)
            pre_ref[slot, r0:, c:c + t] = lb_ref[slot, r0:, c:c + t] + cum[:, :t]
            tot_ref[slot, r0:, c:c + t] = cum[:, t:]

    def weights(step):
        j, d, diag, _, slot = step
        r0 = d * t
        n = Q_TILE - r0
        pre_s, tot_s = pre_ref.at[slot], tot_ref.at[slot]
        if diag:
            a = jnp.where(causal, jnp.exp2(pre_s[r0:r0 + t, :]), 0.0)
            carry_ref[r0:r0 + t, :] = tot_s[r0:r0 + t, :]
            if n > t:
                carry = carry_ref[r0 + t:, :]
                a = jnp.concatenate([a, jnp.exp2(pre_s[r0 + t:, :] + carry)], axis=0)
                carry_ref[r0 + t:, :] = carry + tot_s[r0 + t:, :]
        else:
            carry = carry_ref[...]
            a = jnp.exp2(pre_s[...] + carry)
            carry_ref[...] = carry + tot_s[...]
        av = jnp.dot(a.astype(BF16), vbd_ref[j], preferred_element_type=F32)
        if diag:
            acc_ref[r0:r0 + t, :] = av[:t]
            if n > t:
                acc_ref[r0 + t:, :] += av[t:]
        else:
            acc_ref[...] += av

    stages = (qk, scores, cumsum, weights)
    n_stages = len(stages)

    steps = []
    for qi in range(n_qtiles):
        jd = qi * sub
        tile_steps = [(jd + d, d, True, qi * Q_TILE) for d in reversed(range(sub))]
        tile_steps += [(j, 0, False, qi * Q_TILE) for j in reversed(range(jd))]
        steps += [(st + ((len(steps) + i) % 2,), i == len(tile_steps) - 1)
                  for i, st in enumerate(tile_steps)]
    for g in range(len(steps) + n_stages - 1):
        for s in reversed(range(n_stages)):
            if 0 <= g - s < len(steps):
                st, last_of_tile = steps[g - s]
                stages[s](st)
                if last_of_tile and s == n_stages - 1:
                    o_ref[st[3]:st[3] + Q_TILE, :] = acc_ref[...]


def _sb_attn(qkv, q_norm_g, k_norm_g, bsz, seq, cast_weights):
    rows = qkv.shape[0]
    t = ATT_TILE
    n_pairs = SB_DIM // LANES
    n_tiles = seq // t
    n_steps = bsz * n_pairs
    slabs = [w.shape[0] // n_steps for w in cast_weights]
    assert all(w.shape[0] == s * n_steps and s % 16 == 0 for w, s in zip(cast_weights, slabs))
    slab_specs = [pl.BlockSpec((s, w.shape[1]), lambda b, p: (b * n_pairs + p, 0))
                  for w, s in zip(cast_weights, slabs)]
    j_idx = jnp.arange(t)[:, None]
    s_idx = jnp.arange(2 * t)[None, :]
    tri = ((s_idx >= t) | (j_idx > s_idx)).astype(BF16)
    lane_head = jnp.arange(LANES) // SB_HEAD_DIM
    hsum = (jnp.tile(lane_head, 2)[:, None] == lane_head[None, :]).astype(BF16)
    reps = LANES // SB_HEAD_DIM
    qg = jnp.tile(q_norm_g, reps).reshape(1, LANES)
    kg = jnp.tile(k_norm_g, reps).reshape(1, LANES)
    return pl.pallas_call(
        functools.partial(_sb_attn_kernel, len(cast_weights)),
        grid=(bsz, n_pairs),
        in_specs=[pl.BlockSpec((seq, LANES), lambda b, p: (b, p)),
                  pl.BlockSpec((seq, LANES), lambda b, p: (b, n_pairs + p)),
                  pl.BlockSpec((seq, LANES), lambda b, p: (b, 2 * n_pairs + p)),
                  _const_spec((1, LANES)),
                  _const_spec((1, LANES)),
                  _const_spec((t, 2 * t)),
                  _const_spec((2 * LANES, LANES))] + slab_specs,
        out_specs=[pl.BlockSpec((seq, LANES), lambda b, p: (b, p))] + slab_specs,
        out_shape=[jax.ShapeDtypeStruct((rows, SB_DIM), F32)]
        + [jax.ShapeDtypeStruct(w.shape, BF16) for w in cast_weights],
        scratch_shapes=[pltpu.VMEM((n_tiles, LANES, 2 * t), BF16),
                        pltpu.VMEM((n_tiles, 2 * t, LANES), BF16),
                        pltpu.VMEM((seq, LANES), BF16),
                        pltpu.VMEM((Q_TILE, LANES), F32),
                        pltpu.VMEM((Q_TILE, 2 * t), F32),
                        pltpu.VMEM((2, Q_TILE, 2 * t), F32),
                        pltpu.VMEM((2, Q_TILE, 2 * t), F32),
                        pltpu.VMEM((2, 2, Q_TILE, t), BF16),
                        pltpu.VMEM((2, Q_TILE, 2 * t), F32),
                        pltpu.VMEM((2, Q_TILE, 2 * t), F32)],
        compiler_params=pltpu.CompilerParams(dimension_semantics=("parallel", "parallel"),
                                             vmem_limit_bytes=VMEM_LIMIT),
        name="sb_attn",
    )(qkv, qkv, qkv, qg, kg, tri, hsum, *cast_weights)


def _mix_ffn_kernel(mix_ref, ya_ref, x_ref, mod_ref, ag_ref, wo_ref, g2_ref, w1_ref, w2_ref,
                    o_ref):
    d_ff = w1_ref.shape[1]
    yb = _rms(ya_ref[...], ag_ref[...]).astype(BF16)
    o = (jnp.dot(mix_ref[...], wo_ref[:CONV_DIM, :], preferred_element_type=F32)
         + jnp.dot(yb, wo_ref[CONV_DIM:, :], preferred_element_type=F32))
    x1 = x_ref[...] + mod_ref[2:3, :] * o
    h2 = (_rms(x1, g2_ref[...]) * (1.0 + mod_ref[4:5, :]) + mod_ref[3:4, :]).astype(BF16)
    f = None
    for c0 in range(0, d_ff, FFN_CHUNK):
        u = jnp.dot(h2, w1_ref[:, c0:c0 + FFN_CHUNK], preferred_element_type=F32)
        r = jnp.maximum(u, 0.0)
        fc = jnp.dot((r * r).astype(BF16), w2_ref[c0:c0 + FFN_CHUNK, :],
                     preferred_element_type=F32)
        f = fc if f is None else f + fc
    o_ref[...] = x1 + mod_ref[5:6, :] * f


def _mix_ffn(mix_a, y_attn, x2, mod3, attn_out_g, w_out_b, norm2_g, w1_b, w2_b, seq):
    rows, d = x2.shape
    tm = ROW_TILE
    tiles_per_seq = seq // tm
    return pl.pallas_call(
        _mix_ffn_kernel,
        grid=(rows // tm,),
        in_specs=[pl.BlockSpec((tm, CONV_DIM), lambda i: (i, 0)),
                  pl.BlockSpec((tm, SB_DIM), lambda i: (i, 0)),
                  pl.BlockSpec((tm, d), lambda i: (i, 0)),
                  pl.BlockSpec((None, N_MOD, d), lambda i: (i // tiles_per_seq, 0, 0)),
                  _const_spec((1, SB_DIM)),
                  _const_spec(w_out_b.shape),
                  _const_spec((1, d)),
                  _const_spec(w1_b.shape),
                  _const_spec(w2_b.shape)],
        out_specs=pl.BlockSpec((tm, d), lambda i: (i, 0)),
        out_shape=jax.ShapeDtypeStruct((rows, d), F32),
        compiler_params=pltpu.CompilerParams(dimension_semantics=("parallel",),
                                             vmem_limit_bytes=VMEM_LIMIT),
        name="mix_ffn",
    )(mix_a, y_attn, x2, mod3, attn_out_g.reshape(1, SB_DIM), w_out_b, norm2_g.reshape(1, d),
      w1_b, w2_b)


def kernel(x, c, w_ada, b_ada, norm1_g, w_in, conv_w, q_norm_g, k_norm_g, conv_out_g,
           attn_out_g, w_out, norm2_g, w_ff1, w_ff2):
    bsz, seq, d = x.shape
    assert seq % IN_ROW_TILE == 0 and seq % ROW_TILE == 0 and seq % Q_TILE == 0
    assert Q_TILE % ATT_TILE == 0 and (seq // ATT_TILE) % PREP_TILES == 0
    assert w_in.shape[1] == 3 * CONV_DIM + 3 * SB_DIM
    x2 = x.reshape(bsz * seq, d)
    mod3 = _adaln(c, w_ada, b_ada).reshape(bsz, N_MOD, d)
    mix_a, qkv = _in_proj(x2, mod3, norm1_g, w_in, conv_w, conv_out_g, seq)
    y_attn, w_out_b, w1_b, w2_b = _sb_attn(qkv, q_norm_g, k_norm_g, bsz, seq,
                                           (w_out, w_ff1, w_ff2))
    out = _mix_ffn(mix_a, y_attn, x2, mod3, attn_out_g, w_out_b, norm2_g, w1_b, w2_b, seq)
    return out.reshape(bsz, seq, d)
```

```python
import functools

import jax
import jax.numpy as jnp
import numpy as np
from jax import lax
from jax.experimental import pallas as pl
from jax.experimental.pallas import tpu as pltpu

EPS = 1e-6
CONV_DIM = 512
SB_DIM = 512
SB_HEAD_DIM = 64
N_MOD = 6
LANES = 128
ATT_TILE = 128
Q_TILE = 512
PREP_TILES = 4
LOG2E = 1.4426950408889634
ADALN_COLS = 1024
ROW_TILE = 512
IN_ROW_TILE = 1024
IN_SUB_ROWS = 256
FFN_CHUNK = 512
VMEM_LIMIT = 56 * 1024 * 1024

F32 = jnp.float32
BF16 = jnp.bfloat16


def _split_bf16(a):
    hi = a.astype(BF16)
    lo = (a - hi.astype(F32)).astype(BF16)
    return hi, lo


def _const_spec(shape):
    return pl.BlockSpec(shape, lambda *_: (0,) * len(shape), pipeline_mode=pl.Buffered(1))


def _adaln_kernel(c_ref, w_ref, b_ref, o_ref):
    c = c_ref[...]
    s = c * (1.0 / (1.0 + jnp.exp(-c)))
    s_hi, s_lo = _split_bf16(s)
    w_hi, w_lo = _split_bf16(w_ref[...])
    bsz = c.shape[0]
    dot = functools.partial(jnp.dot, preferred_element_type=F32)
    both = dot(jnp.concatenate([s_hi, s_lo], axis=0), w_hi)
    o_ref[...] = both[:bsz] + (dot(s_hi, w_lo) + both[bsz:]) + b_ref[...]


def _adaln(c, w_ada, b_ada):
    bsz, d = c.shape
    n = w_ada.shape[1]
    tn = ADALN_COLS
    return pl.pallas_call(
        _adaln_kernel,
        grid=(n // tn,),
        in_specs=[pl.BlockSpec((bsz, d), lambda j: (0, 0)),
                  pl.BlockSpec((d, tn), lambda j: (0, j)),
                  pl.BlockSpec((1, tn), lambda j: (0, j))],
        out_specs=pl.BlockSpec((bsz, tn), lambda j: (0, j)),
        out_shape=jax.ShapeDtypeStruct((bsz, n), F32),
        compiler_params=pltpu.CompilerParams(dimension_semantics=("parallel",),
                                             vmem_limit_bytes=VMEM_LIMIT),
        name="adaln",
    )(c, w_ada, b_ada.reshape(1, n))


def _rms(xf, g):
    ms = jnp.mean(xf * xf, axis=-1, keepdims=True)
    return xf * lax.rsqrt(ms + EPS) * g


def _in_proj_kernel(tiles_per_seq, x_ref, mod_ref, g1_ref, w32_ref, cw_ref, cg_ref,
                    mix_ref, qkv_ref, ext_ref, hb_ref, p_ref, w_ref):
    tm = x_ref.shape[0]
    i = pl.program_id(0)

    @pl.when(i == 0)
    def _():
        w_ref[...] = w32_ref[...].astype(BF16)

    @pl.when(i % tiles_per_seq == 0)
    def _():
        ext_ref[0:8, :] = jnp.zeros((8, CONV_DIM), F32)

    @pl.when(i % tiles_per_seq != 0)
    def _():
        ext_ref[0:8, :] = ext_ref[tm:tm + 8, :]

    shift = mod_ref[0:1, :]
    scale = mod_ref[1:2, :]
    n_conv = 3 * CONV_DIM
    n_sub = tm // IN_SUB_ROWS

    def rows(k):
        return slice(k * IN_SUB_ROWS, (k + 1) * IN_SUB_ROWS)

    def norm(k):
        h = _rms(x_ref[rows(k), :], g1_ref[...]) * (1.0 + scale) + shift
        hb_ref[rows(k), :] = h.astype(BF16)

    def project(k):
        hb = hb_ref[rows(k), :]
        p_ref[k % 2] = jnp.dot(hb, w_ref[:, :n_conv], preferred_element_type=F32)
        qkv_ref[rows(k), :] = jnp.dot(hb, w_ref[:, n_conv:], preferred_element_type=F32)

    def conv_mixer(k):
        r0 = k * IN_SUB_ROWS
        r1 = r0 + IN_SUB_ROWS
        b_gate = p_ref[k % 2, :, :CONV_DIM]
        cu = p_ref[k % 2, :, CONV_DIM:2 * CONV_DIM] * p_ref[k % 2, :, 2 * CONV_DIM:]
        ext_ref[r0 + 8:r1 + 8, :] = cu
        conv = (cw_ref[2:3, :] * cu + cw_ref[1:2, :] * ext_ref[r0 + 7:r1 + 7, :]
                + cw_ref[0:1, :] * ext_ref[r0 + 6:r1 + 6, :])
        mix_ref[rows(k), :] = _rms(b_gate * conv, cg_ref[...]).astype(BF16)

    norm(0)
    for k in range(n_sub):
        if k + 1 < n_sub:
            norm(k + 1)
        if k >= 1:
            conv_mixer(k - 1)
        project(k)
    conv_mixer(n_sub - 1)


def _in_proj(x2, mod3, norm1_g, w_in, conv_w, conv_out_g, seq):
    rows, d = x2.shape
    tm = IN_ROW_TILE
    tiles_per_seq = seq // tm
    n_in = w_in.shape[1]
    n_qkv = n_in - 3 * CONV_DIM
    return pl.pallas_call(
        functools.partial(_in_proj_kernel, tiles_per_seq),
        grid=(rows // tm,),
        in_specs=[pl.BlockSpec((tm, d), lambda i: (i, 0)),
                  pl.BlockSpec((None, N_MOD, d), lambda i: (i // tiles_per_seq, 0, 0)),
                  _const_spec((1, d)),
                  _const_spec((d, n_in)),
                  _const_spec(conv_w.shape),
                  _const_spec((1, CONV_DIM))],
        out_specs=[pl.BlockSpec((tm, CONV_DIM), lambda i: (i, 0)),
                   pl.BlockSpec((tm, n_qkv), lambda i: (i, 0))],
        out_shape=[jax.ShapeDtypeStruct((rows, CONV_DIM), BF16),
                   jax.ShapeDtypeStruct((rows, n_qkv), F32)],
        scratch_shapes=[pltpu.VMEM((tm + 8, CONV_DIM), F32),
                        pltpu.VMEM((tm, d), BF16),
                        pltpu.VMEM((2, IN_SUB_ROWS, 3 * CONV_DIM), F32),
                        pltpu.VMEM((d, n_in), BF16)],
        compiler_params=pltpu.CompilerParams(dimension_semantics=("arbitrary",),
                                             vmem_limit_bytes=VMEM_LIMIT),
        name="in_proj",
    )(x2, mod3, norm1_g.reshape(1, d), w_in, conv_w, conv_out_g.reshape(1, CONV_DIM))


def _head_rms(a, g, hsum):
    s_hi, s_lo = _split_bf16(a * a)
    ssq = jnp.dot(jnp.concatenate([s_hi, s_lo], axis=1), hsum, preferred_element_type=F32)
    return a * lax.rsqrt(ssq * (1.0 / SB_HEAD_DIM) + EPS) * g


def _sb_attn_kernel(n_cast, q_ref, k_ref, v_ref, qg_ref, kg_ref, tri_ref, hsum_ref, *refs):
    w32_refs, o_ref, w16_refs = refs[:n_cast], refs[n_cast], refs[n_cast + 1:2 * n_cast + 1]
    (kbd_ref, vbd_ref, qn_ref, acc_ref, carry_ref,
     z_ref, lhs_ref, pre_ref, tot_ref) = refs[2 * n_cast + 1:]
    for w32_ref, w16_ref in zip(w32_refs, w16_refs):
        w16_ref[...] = w32_ref[...].astype(BF16)

    t = ATT_TILE
    sub = Q_TILE // t
    n_ktiles = k_ref.shape[0] // t
    n_qtiles = q_ref.shape[0] // Q_TILE
    lo_lane = lax.broadcasted_iota(jnp.int32, (t, LANES), 1) < SB_HEAD_DIM
    lo_row = lax.broadcasted_iota(jnp.int32, (LANES, t), 0) < SB_HEAD_DIM
    row = lax.broadcasted_iota(jnp.int32, (t, 2 * t), 0)
    col = lax.broadcasted_iota(jnp.int32, (t, 2 * t), 1) & (t - 1)
    causal = col < row

    def prep(c, _):
        for i in range(PREP_TILES):
            j = c * PREP_TILES + i
            r0 = pl.multiple_of(j * t, t)
            kt = _head_rms(k_ref[pl.ds(r0, t), :], kg_ref[...], hsum_ref[...]).T
            kbd_ref[j, :, :t] = jnp.where(lo_row, kt, 0.0).astype(BF16)
            kbd_ref[j, :, t:] = jnp.where(lo_row, 0.0, kt).astype(BF16)
            v = v_ref[pl.ds(r0, t), :]
            vbd_ref[j, :t, :] = jnp.where(lo_lane, v, 0.0).astype(BF16)
            vbd_ref[j, t:, :] = jnp.where(lo_lane, 0.0, v).astype(BF16)
            qn = _head_rms(q_ref[pl.ds(r0, t), :], qg_ref[...], hsum_ref[...])
            qn_ref[pl.ds(r0, t), :] = qn.astype(BF16)
        return 0

    lax.fori_loop(0, n_ktiles // PREP_TILES, prep, 0)


    def qk(step):
        j, d, _, q0, slot = step
        q = qn_ref[pl.ds(q0 + d * t, Q_TILE - d * t), :]
        z_ref[slot, d * t:, :] = jnp.dot(q, kbd_ref[j], preferred_element_type=F32)

    def scores(step):
        _, d, diag, _, slot = step
        r0 = d * t
        n = Q_TILE - r0
        z = z_ref[slot, r0:, :]
        sp = jnp.log2(1.0 + jnp.exp2(-jnp.abs(z)))
        log_beta = jnp.minimum(z, 0.0) - sp
        log_1mb = log_beta - z
        if diag:
            masked = jnp.where(causal, log_1mb[:t], 0.0)
            log_1mb = masked if n == t else jnp.concatenate([masked, log_1mb[t:]], axis=0)
        l16 = log_1mb.astype(BF16)
        b16 = log_beta.astype(BF16)
        for h in range(2):
            c = h * t
            lhs_ref[slot, h, r0:, :t] = l16[:, c:c + t]
            lhs_ref[slot, h, r0:, t:] = b16[:, c:c + t]

    def cumsum(step):
        r0 = step[1] * t
        slot = step[4]
        for h in range(2):
            c = h * t
            cum = jnp.dot(lhs_ref[slot, h, r0:, :], tri_ref[...], preferred_element_type=F32)
            pre_ref[slot, r0:, c:c + t] = cum[:, :t]
            tot_ref[slot, r0:, c:c + t] = cum[:, t:]

    def weights(step):
        j, d, diag, _, slot = step
        r0 = d * t
        n = Q_TILE - r0
        pre_s, tot_s = pre_ref.at[slot], tot_ref.at[slot]
        if diag:
            a = jnp.where(causal, jnp.exp2(pre_s[r0:r0 + t, :]), 0.0)
            carry_ref[r0:r0 + t, :] = tot_s[r0:r0 + t, :]
            if n > t:
                carry = carry_ref[r0 + t:, :]
                a = jnp.concatenate([a, jnp.exp2(pre_s[r0 + t:, :] + carry)], axis=0)
                carry_ref[r0 + t:, :] = carry + tot_s[r0 + t:, :]
        else:
            carry = carry_ref[...]
            a = jnp.exp2(pre_s[...] + carry)
            carry_ref[...] = carry + tot_s[...]
        av = jnp.dot(a.astype(BF16), vbd_ref[j], preferred_element_type=F32)
        if diag:
            acc_ref[r0:r0 + t, :] = av[:t]
            if n > t:
                acc_ref[r0 + t:, :] += av[t:]
        else:
            acc_ref[...] += av

    stages = (qk, scores, cumsum, weights)
    n_stages = len(stages)

    steps = []
    for qi in range(n_qtiles):
        jd = qi * sub
        tile_steps = [(jd + d, d, True, qi * Q_TILE) for d in reversed(range(sub))]
        tile_steps += [(j, 0, False, qi * Q_TILE) for j in reversed(range(jd))]
        steps += [(st + ((len(steps) + i) % 2,), i == len(tile_steps) - 1)
                  for i, st in enumerate(tile_steps)]
    for g in range(len(steps) + n_stages - 1):
        for s in reversed(range(n_stages)):
            if 0 <= g - s < len(steps):
                st, last_of_tile = steps[g - s]
                stages[s](st)
                if last_of_tile and s == n_stages - 1:
                    o_ref[st[3]:st[3] + Q_TILE, :] = acc_ref[...]


def _sb_attn(qkv, q_norm_g, k_norm_g, bsz, seq, cast_weights):
    rows = qkv.shape[0]
    t = ATT_TILE
    n_pairs = SB_DIM // LANES
    n_tiles = seq // t
    n_steps = bsz * n_pairs
    slabs = [w.shape[0] // n_steps for w in cast_weights]
    assert all(w.shape[0] == s * n_steps and s % 16 == 0 for w, s in zip(cast_weights, slabs))
    slab_specs = [pl.BlockSpec((s, w.shape[1]), lambda b, p: (b * n_pairs + p, 0))
                  for w, s in zip(cast_weights, slabs)]
    j_idx = np.arange(t)[:, None]
    s_idx = np.arange(t)[None, :]
    top = np.concatenate([j_idx > s_idx, np.ones((t, t), bool)], axis=1)
    bottom = np.concatenate([np.eye(t, dtype=bool), np.zeros((t, t), bool)], axis=1)
    tri = jnp.asarray(np.concatenate([top, bottom], axis=0), BF16)
    lane_head = np.arange(LANES) // SB_HEAD_DIM
    hsum = jnp.asarray(np.tile(lane_head, 2)[:, None] == lane_head[None, :], BF16)
    reps = LANES // SB_HEAD_DIM
    qg = (jnp.tile(q_norm_g, reps) * (SB_HEAD_DIM ** -0.5 * LOG2E)).reshape(1, LANES)
    kg = jnp.tile(k_norm_g, reps).reshape(1, LANES)
    return pl.pallas_call(
        functools.partial(_sb_attn_kernel, len(cast_weights)),
        grid=(bsz, n_pairs),
        in_specs=[pl.BlockSpec((seq, LANES), lambda b, p: (b, p)),
                  pl.BlockSpec((seq, LANES), lambda b, p: (b, n_pairs + p)),
                  pl.BlockSpec((seq, LANES), lambda b, p: (b, 2 * n_pairs + p)),
                  _const_spec((1, LANES)),
                  _const_spec((1, LANES)),
                  _const_spec((2 * t, 2 * t)),
                  _const_spec((2 * LANES, LANES))] + slab_specs,
        out_specs=[pl.BlockSpec((seq, LANES), lambda b, p: (b, p))] + slab_specs,
        out_shape=[jax.ShapeDtypeStruct((rows, SB_DIM), F32)]
        + [jax.ShapeDtypeStruct(w.shape, BF16) for w in cast_weights],
        scratch_shapes=[pltpu.VMEM((n_tiles, LANES, 2 * t), BF16),
                        pltpu.VMEM((n_tiles, 2 * t, LANES), BF16),
                        pltpu.VMEM((seq, LANES), BF16),
                        pltpu.VMEM((Q_TILE, LANES), F32),
                        pltpu.VMEM((Q_TILE, 2 * t), F32),
                        pltpu.VMEM((2, Q_TILE, 2 * t), F32),
                        pltpu.VMEM((2, 2, Q_TILE, 2 * t), BF16),
                        pltpu.VMEM((2, Q_TILE, 2 * t), F32),
                        pltpu.VMEM((2, Q_TILE, 2 * t), F32)],
        compiler_params=pltpu.CompilerParams(dimension_semantics=("parallel", "parallel"),
                                             vmem_limit_bytes=VMEM_LIMIT),
        name="sb_attn",
    )(qkv, qkv, qkv, qg, kg, tri, hsum, *cast_weights)


def _mix_ffn_kernel(mix_ref, ya_ref, x_ref, mod_ref, ag_ref, wo_ref, g2_ref, w1_ref, w2_ref,
                    o_ref):
    d_ff = w1_ref.shape[1]
    yb = _rms(ya_ref[...], ag_ref[...]).astype(BF16)
    o = (jnp.dot(mix_ref[...], wo_ref[:CONV_DIM, :], preferred_element_type=F32)
         + jnp.dot(yb, wo_ref[CONV_DIM:, :], preferred_element_type=F32))
    x1 = x_ref[...] + mod_ref[2:3, :] * o
    h2 = (_rms(x1, g2_ref[...]) * (1.0 + mod_ref[4:5, :]) + mod_ref[3:4, :]).astype(BF16)
    f = None
    for c0 in range(0, d_ff, FFN_CHUNK):
        u = jnp.dot(h2, w1_ref[:, c0:c0 + FFN_CHUNK], preferred_element_type=F32)
        r = jnp.maximum(u, 0.0)
        fc = jnp.dot((r * r).astype(BF16), w2_ref[c0:c0 + FFN_CHUNK, :],
                     preferred_element_type=F32)
        f = fc if f is None else f + fc
    o_ref[...] = x1 + mod_ref[5:6, :] * f


def _mix_ffn(mix_a, y_attn, x2, mod3, attn_out_g, w_out_b, norm2_g, w1_b, w2_b, seq):
    rows, d = x2.shape
    tm = ROW_TILE
    tiles_per_seq = seq // tm
    return pl.pallas_call(
        _mix_ffn_kernel,
        grid=(rows // tm,),
        in_specs=[pl.BlockSpec((tm, CONV_DIM), lambda i: (i, 0)),
                  pl.BlockSpec((tm, SB_DIM), lambda i: (i, 0)),
                  pl.BlockSpec((tm, d), lambda i: (i, 0)),
                  pl.BlockSpec((None, N_MOD, d), lambda i: (i // tiles_per_seq, 0, 0)),
                  _const_spec((1, SB_DIM)),
                  _const_spec(w_out_b.shape),
                  _const_spec((1, d)),
                  _const_spec(w1_b.shape),
                  _const_spec(w2_b.shape)],
        out_specs=pl.BlockSpec((tm, d), lambda i: (i, 0)),
        out_shape=jax.ShapeDtypeStruct((rows, d), F32),
        compiler_params=pltpu.CompilerParams(dimension_semantics=("parallel",),
                                             vmem_limit_bytes=VMEM_LIMIT),
        name="mix_ffn",
    )(mix_a, y_attn, x2, mod3, attn_out_g.reshape(1, SB_DIM), w_out_b, norm2_g.reshape(1, d),
      w1_b, w2_b)


def kernel(x, c, w_ada, b_ada, norm1_g, w_in, conv_w, q_norm_g, k_norm_g, conv_out_g,
           attn_out_g, w_out, norm2_g, w_ff1, w_ff2):
    bsz, seq, d = x.shape
    assert seq % IN_ROW_TILE == 0 and seq % ROW_TILE == 0 and seq % Q_TILE == 0
    assert Q_TILE % ATT_TILE == 0 and (seq // ATT_TILE) % PREP_TILES == 0
    assert w_in.shape[1] == 3 * CONV_DIM + 3 * SB_DIM
    x2 = x.reshape(bsz * seq, d)
    mod3 = _adaln(c, w_ada, b_ada).reshape(bsz, N_MOD, d)
    mix_a, qkv = _in_proj(x2, mod3, norm1_g, w_in, conv_w, conv_out_g, seq)
    y_attn, w_out_b, w1_b, w2_b = _sb_attn(qkv, q_norm_g, k_norm_g, bsz, seq,
                                           (w_out, w_ff1, w_ff2))
    out = _mix_ffn(mix_a, y_attn, x2, mod3, attn_out_g, w_out_b, norm2_g, w1_b, w2_b, seq)
    return out.reshape(bsz, seq, d)
```

```python
import functools

import jax
import jax.numpy as jnp
import numpy as np
from jax import lax
from jax.experimental import pallas as pl
from jax.experimental.pallas import tpu as pltpu

EPS = 1e-6
CONV_DIM = 512
SB_DIM = 512
SB_HEAD_DIM = 64
N_MOD = 6
LANES = 128
HALO = 8
ATT_TILE = 128
Q_TILE = 512
PREP_TILES = 4
LOG2E = 1.4426950408889634
ADALN_COLS = 1024
ROW_TILE = 512
IN_ROW_TILE = 1024
IN_SUB_ROWS = 256
FFN_CHUNK = 512
VMEM_LIMIT = 56 * 1024 * 1024

F32 = jnp.float32
BF16 = jnp.bfloat16


def _split_bf16(a):
    hi = a.astype(BF16)
    lo = (a - hi.astype(F32)).astype(BF16)
    return hi, lo


def _const_spec(shape):
    return pl.BlockSpec(shape, lambda *_: (0,) * len(shape), pipeline_mode=pl.Buffered(1))


def _adaln_kernel(c_ref, w_ref, b_ref, o_ref):
    c = c_ref[...]
    s = c * (1.0 / (1.0 + jnp.exp(-c)))
    s_hi, s_lo = _split_bf16(s)
    w_hi, w_lo = _split_bf16(w_ref[...])
    bsz = c.shape[0]
    dot = functools.partial(jnp.dot, preferred_element_type=F32)
    both = dot(jnp.concatenate([s_hi, s_lo], axis=0), w_hi)
    o_ref[...] = both[:bsz] + (dot(s_hi, w_lo) + both[bsz:]) + b_ref[...]


def _adaln(c, w_ada, b_ada):
    bsz, d = c.shape
    n = w_ada.shape[1]
    tn = ADALN_COLS
    return pl.pallas_call(
        _adaln_kernel,
        grid=(n // tn,),
        in_specs=[pl.BlockSpec((bsz, d), lambda j: (0, 0)),
                  pl.BlockSpec((d, tn), lambda j: (0, j)),
                  pl.BlockSpec((1, tn), lambda j: (0, j))],
        out_specs=pl.BlockSpec((bsz, tn), lambda j: (0, j)),
        out_shape=jax.ShapeDtypeStruct((bsz, n), F32),
        compiler_params=pltpu.CompilerParams(dimension_semantics=("parallel",),
                                             vmem_limit_bytes=VMEM_LIMIT),
        name="adaln",
    )(c, w_ada, b_ada.reshape(1, n))


def _rms(xf, g):
    ms = jnp.mean(xf * xf, axis=-1, keepdims=True)
    return xf * lax.rsqrt(ms + EPS) * g


def _in_proj_kernel(tiles_per_seq, x_ref, mod_ref, g1_ref, w32_ref, cw_ref, cg_ref,
                    mix_ref, qkv_ref, ext_ref, hb_ref, p_ref, w_ref):
    tm = x_ref.shape[0]
    i = pl.program_id(0)

    @pl.when(i == 0)
    def _():
        w_ref[...] = w32_ref[...].astype(BF16)

    @pl.when(i % tiles_per_seq == 0)
    def _():
        ext_ref[0:HALO, :] = jnp.zeros((HALO, CONV_DIM), F32)

    @pl.when(i % tiles_per_seq != 0)
    def _():
        ext_ref[0:HALO, :] = ext_ref[tm:tm + HALO, :]

    shift = mod_ref[0:1, :]
    gain = g1_ref[...] * (1.0 + mod_ref[1:2, :])
    n_conv = 3 * CONV_DIM
    n_sub = tm // IN_SUB_ROWS

    def rows(k):
        return slice(k * IN_SUB_ROWS, (k + 1) * IN_SUB_ROWS)

    def norm(k):
        hb_ref[rows(k), :] = (_rms(x_ref[rows(k), :], gain) + shift).astype(BF16)

    def project(k):
        hb = hb_ref[rows(k), :]
        p_ref[k % 2] = jnp.dot(hb, w_ref[:, :n_conv], preferred_element_type=F32)
        qkv_ref[rows(k), :] = jnp.dot(hb, w_ref[:, n_conv:], preferred_element_type=F32)

    def conv_mixer(k):
        r0 = k * IN_SUB_ROWS
        r1 = r0 + IN_SUB_ROWS
        b_gate = p_ref[k % 2, :, :CONV_DIM]
        cu = p_ref[k % 2, :, CONV_DIM:2 * CONV_DIM] * p_ref[k % 2, :, 2 * CONV_DIM:]
        ext_ref[r0 + HALO:r1 + HALO, :] = cu
        conv = (cw_ref[2:3, :] * cu + cw_ref[1:2, :] * ext_ref[r0 + HALO - 1:r1 + HALO - 1, :]
                + cw_ref[0:1, :] * ext_ref[r0 + HALO - 2:r1 + HALO - 2, :])
        mix_ref[rows(k), :] = _rms(b_gate * conv, cg_ref[...]).astype(BF16)

    norm(0)
    for k in range(n_sub):
        if k + 1 < n_sub:
            norm(k + 1)
        if k >= 1:
            conv_mixer(k - 1)
        project(k)
    conv_mixer(n_sub - 1)


def _in_proj(x2, mod3, norm1_g, w_in, conv_w, conv_out_g, seq):
    rows, d = x2.shape
    tm = IN_ROW_TILE
    tiles_per_seq = seq // tm
    n_in = w_in.shape[1]
    n_qkv = n_in - 3 * CONV_DIM
    return pl.pallas_call(
        functools.partial(_in_proj_kernel, tiles_per_seq),
        grid=(rows // tm,),
        in_specs=[pl.BlockSpec((tm, d), lambda i: (i, 0)),
                  pl.BlockSpec((None, N_MOD, d), lambda i: (i // tiles_per_seq, 0, 0)),
                  _const_spec((1, d)),
                  _const_spec((d, n_in)),
                  _const_spec(conv_w.shape),
                  _const_spec((1, CONV_DIM))],
        out_specs=[pl.BlockSpec((tm, CONV_DIM), lambda i: (i, 0)),
                   pl.BlockSpec((tm, n_qkv), lambda i: (i, 0))],
        out_shape=[jax.ShapeDtypeStruct((rows, CONV_DIM), BF16),
                   jax.ShapeDtypeStruct((rows, n_qkv), F32)],
        scratch_shapes=[pltpu.VMEM((tm + HALO, CONV_DIM), F32),
                        pltpu.VMEM((tm, d), BF16),
                        pltpu.VMEM((2, IN_SUB_ROWS, 3 * CONV_DIM), F32),
                        pltpu.VMEM((d, n_in), BF16)],
        compiler_params=pltpu.CompilerParams(dimension_semantics=("arbitrary",),
                                             vmem_limit_bytes=VMEM_LIMIT),
        name="in_proj",
    )(x2, mod3, norm1_g.reshape(1, d), w_in, conv_w, conv_out_g.reshape(1, CONV_DIM))


def _head_rms(a, g, hsum):
    s_hi, s_lo = _split_bf16(a * a)
    ssq = jnp.dot(jnp.concatenate([s_hi, s_lo], axis=1), hsum, preferred_element_type=F32)
    return a * lax.rsqrt(ssq * (1.0 / SB_HEAD_DIM) + EPS) * g


def _sb_attn_kernel(n_cast, q_ref, k_ref, v_ref, qg_ref, kg_ref, tri_ref, hsum_ref, *refs):
    w32_refs, o_ref, w16_refs = refs[:n_cast], refs[n_cast], refs[n_cast + 1:2 * n_cast + 1]
    (kbd_ref, vbd_ref, qn_ref, acc_ref, carry_ref,
     z_ref, lb_ref, lst_ref, pre_ref, tot_ref) = refs[2 * n_cast + 1:]
    for w32_ref, w16_ref in zip(w32_refs, w16_refs):
        w16_ref[...] = w32_ref[...].astype(BF16)

    t = ATT_TILE
    sub = Q_TILE // t
    n_ktiles = k_ref.shape[0] // t
    n_qtiles = q_ref.shape[0] // Q_TILE
    lo_lane = lax.broadcasted_iota(jnp.int32, (t, LANES), 1) < SB_HEAD_DIM
    lo_row = lax.broadcasted_iota(jnp.int32, (LANES, t), 0) < SB_HEAD_DIM
    row = lax.broadcasted_iota(jnp.int32, (t, 2 * t), 0)
    col = lax.broadcasted_iota(jnp.int32, (t, 2 * t), 1) & (t - 1)
    causal = col < row

    def prep(c, _):
        for i in range(PREP_TILES):
            j = c * PREP_TILES + i
            r0 = pl.multiple_of(j * t, t)
            kt = _head_rms(k_ref[pl.ds(r0, t), :], kg_ref[...], hsum_ref[...]).T
            kbd_ref[j, :, :t] = jnp.where(lo_row, kt, 0.0).astype(BF16)
            kbd_ref[j, :, t:] = jnp.where(lo_row, 0.0, kt).astype(BF16)
            v = v_ref[pl.ds(r0, t), :]
            vbd_ref[j, :t, :] = jnp.where(lo_lane, v, 0.0).astype(BF16)
            vbd_ref[j, t:, :] = jnp.where(lo_lane, 0.0, v).astype(BF16)
            qn = _head_rms(q_ref[pl.ds(r0, t), :], qg_ref[...], hsum_ref[...])
            qn_ref[pl.ds(r0, t), :] = qn.astype(BF16)
        return 0

    lax.fori_loop(0, n_ktiles // PREP_TILES, prep, 0)


    def qk(step):
        j, d, _, q0, slot = step
        q = qn_ref[pl.ds(q0 + d * t, Q_TILE - d * t), :]
        z_ref[slot, d * t:, :] = jnp.dot(q, kbd_ref[j], preferred_element_type=F32)

    def scores(step):
        _, d, diag, _, slot = step
        r0 = d * t
        n = Q_TILE - r0
        z = z_ref[slot, r0:, :]
        sp = jnp.log2(1.0 + jnp.exp2(-jnp.abs(z)))
        log_beta = jnp.minimum(z, 0.0) - sp
        log_1mb = log_beta - z
        if diag:
            masked = jnp.where(causal, log_1mb[:t], 0.0)
            log_1mb = masked if n == t else jnp.concatenate([masked, log_1mb[t:]], axis=0)
        lb_ref[slot, r0:, :] = log_beta
        l16 = log_1mb.astype(BF16)
        lst_ref[slot, 0, r0:, :] = l16[:, :t]
        lst_ref[slot, 1, r0:, :] = l16[:, t:]

    def cumsum(step):
        r0 = step[1] * t
        slot = step[4]
        for h in range(2):
            c = h * t
            cum = jnp.dot(lst_ref[slot, h, r0:, :], tri_ref[...], preferred_element_type=F32)
            pre_ref[slot, r0:, c:c + t] = lb_ref[slot, r0:, c:c + t] + cum[:, :t]
            tot_ref[slot, r0:, c:c + t] = cum[:, t:]

    def weights(step):
        j, d, diag, _, slot = step
        r0 = d * t
        n = Q_TILE - r0
        pre_s, tot_s = pre_ref.at[slot], tot_ref.at[slot]
        if diag:
            a = jnp.where(causal, jnp.exp2(pre_s[r0:r0 + t, :]), 0.0)
            carry_ref[r0:r0 + t, :] = tot_s[r0:r0 + t, :]
            if n > t:
                carry = carry_ref[r0 + t:, :]
                a = jnp.concatenate([a, jnp.exp2(pre_s[r0 + t:, :] + carry)], axis=0)
                carry_ref[r0 + t:, :] = carry + tot_s[r0 + t:, :]
        else:
            carry = carry_ref[...]
            a = jnp.exp2(pre_s[...] + carry)
            carry_ref[...] = carry + tot_s[...]
        av = jnp.dot(a.astype(BF16), vbd_ref[j], preferred_element_type=F32)
        if diag:
            acc_ref[r0:r0 + t, :] = av[:t]
            if n > t:
                acc_ref[r0 + t:, :] += av[t:]
        else:
            acc_ref[...] += av

    stages = (qk, scores, cumsum, weights)
    n_stages = len(stages)

    steps = []
    for qi in range(n_qtiles):
        jd = qi * sub
        tile_steps = [(jd + d, d, True, qi * Q_TILE) for d in reversed(range(sub))]
        tile_steps += [(j, 0, False, qi * Q_TILE) for j in reversed(range(jd))]
        steps += [(st + ((len(steps) + i) % 2,), i == len(tile_steps) - 1)
                  for i, st in enumerate(tile_steps)]
    for g in range(len(steps) + n_stages - 1):
        for s in reversed(range(n_stages)):
            if 0 <= g - s < len(steps):
                st, last_of_tile = steps[g - s]
                stages[s](st)
                if last_of_tile and s == n_stages - 1:
                    o_ref[st[3]:st[3] + Q_TILE, :] = acc_ref[...]


def _sb_attn(qkv, q_norm_g, k_norm_g, bsz, seq, cast_weights):
    rows = qkv.shape[0]
    t = ATT_TILE
    n_pairs = SB_DIM // LANES
    n_tiles = seq // t
    n_steps = bsz * n_pairs
    slabs = [w.shape[0] // n_steps for w in cast_weights]
    assert all(w.shape[0] == s * n_steps and s % 16 == 0 for w, s in zip(cast_weights, slabs))
    slab_specs = [pl.BlockSpec((s, w.shape[1]), lambda b, p: (b * n_pairs + p, 0))
                  for w, s in zip(cast_weights, slabs)]
    j_idx = np.arange(t)[:, None]
    s_idx = np.arange(2 * t)[None, :]
    tri = jnp.asarray((s_idx >= t) | (j_idx > s_idx), BF16)
    lane_head = np.arange(LANES) // SB_HEAD_DIM
    hsum = jnp.asarray(np.tile(lane_head, 2)[:, None] == lane_head[None, :], BF16)
    reps = LANES // SB_HEAD_DIM
    qg = (jnp.tile(q_norm_g, reps) * (SB_HEAD_DIM ** -0.5 * LOG2E)).reshape(1, LANES)
    kg = jnp.tile(k_norm_g, reps).reshape(1, LANES)
    return pl.pallas_call(
        functools.partial(_sb_attn_kernel, len(cast_weights)),
        grid=(bsz, n_pairs),
        in_specs=[pl.BlockSpec((seq, LANES), lambda b, p: (b, p)),
                  pl.BlockSpec((seq, LANES), lambda b, p: (b, n_pairs + p)),
                  pl.BlockSpec((seq, LANES), lambda b, p: (b, 2 * n_pairs + p)),
                  _const_spec((1, LANES)),
                  _const_spec((1, LANES)),
                  _const_spec((t, 2 * t)),
                  _const_spec((2 * LANES, LANES))] + slab_specs,
        out_specs=[pl.BlockSpec((seq, LANES), lambda b, p: (b, p))] + slab_specs,
        out_shape=[jax.ShapeDtypeStruct((rows, SB_DIM), F32)]
        + [jax.ShapeDtypeStruct(w.shape, BF16) for w in cast_weights],
        scratch_shapes=[pltpu.VMEM((n_tiles, LANES, 2 * t), BF16),
                        pltpu.VMEM((n_tiles, 2 * t, LANES), BF16),
                        pltpu.VMEM((seq, LANES), BF16),
                        pltpu.VMEM((Q_TILE, LANES), F32),
                        pltpu.VMEM((Q_TILE, 2 * t), F32),
                        pltpu.VMEM((2, Q_TILE, 2 * t), F32),
                        pltpu.VMEM((2, Q_TILE, 2 * t), F32),
                        pltpu.VMEM((2, 2, Q_TILE, t), BF16),
                        pltpu.VMEM((2, Q_TILE, 2 * t), F32),
                        pltpu.VMEM((2, Q_TILE, 2 * t), F32)],
        compiler_params=pltpu.CompilerParams(dimension_semantics=("parallel", "parallel"),
                                             vmem_limit_bytes=VMEM_LIMIT),
        name="sb_attn",
    )(qkv, qkv, qkv, qg, kg, tri, hsum, *cast_weights)


def _mix_ffn_kernel(mix_ref, ya_ref, x_ref, mod_ref, ag_ref, wo_ref, g2_ref, w1_ref, w2_ref,
                    o_ref):
    d_ff = w1_ref.shape[1]
    yb = _rms(ya_ref[...], ag_ref[...]).astype(BF16)
    o = (jnp.dot(mix_ref[...], wo_ref[:CONV_DIM, :], preferred_element_type=F32)
         + jnp.dot(yb, wo_ref[CONV_DIM:, :], preferred_element_type=F32))
    x1 = x_ref[...] + mod_ref[2:3, :] * o
    gain2 = g2_ref[...] * (1.0 + mod_ref[4:5, :])
    h2 = (_rms(x1, gain2) + mod_ref[3:4, :]).astype(BF16)
    f = None
    for c0 in range(0, d_ff, FFN_CHUNK):
        u = jnp.dot(h2, w1_ref[:, c0:c0 + FFN_CHUNK], preferred_element_type=F32)
        r = jnp.maximum(u, 0.0)
        fc = jnp.dot((r * r).astype(BF16), w2_ref[c0:c0 + FFN_CHUNK, :],
                     preferred_element_type=F32)
        f = fc if f is None else f + fc
    o_ref[...] = x1 + mod_ref[5:6, :] * f


def _mix_ffn(mix_a, y_attn, x2, mod3, attn_out_g, w_out_b, norm2_g, w1_b, w2_b, seq):
    rows, d = x2.shape
    tm = ROW_TILE
    tiles_per_seq = seq // tm
    return pl.pallas_call(
        _mix_ffn_kernel,
        grid=(rows // tm,),
        in_specs=[pl.BlockSpec((tm, CONV_DIM), lambda i: (i, 0)),
                  pl.BlockSpec((tm, SB_DIM), lambda i: (i, 0)),
                  pl.BlockSpec((tm, d), lambda i: (i, 0)),
                  pl.BlockSpec((None, N_MOD, d), lambda i: (i // tiles_per_seq, 0, 0)),
                  _const_spec((1, SB_DIM)),
                  _const_spec(w_out_b.shape),
                  _const_spec((1, d)),
                  _const_spec(w1_b.shape),
                  _const_spec(w2_b.shape)],
        out_specs=pl.BlockSpec((tm, d), lambda i: (i, 0)),
        out_shape=jax.ShapeDtypeStruct((rows, d), F32),
        compiler_params=pltpu.CompilerParams(dimension_semantics=("parallel",),
                                             vmem_limit_bytes=VMEM_LIMIT),
        name="mix_ffn",
    )(mix_a, y_attn, x2, mod3, attn_out_g.reshape(1, SB_DIM), w_out_b, norm2_g.reshape(1, d),
      w1_b, w2_b)


def kernel(x, c, w_ada, b_ada, norm1_g, w_in, conv_w, q_norm_g, k_norm_g, conv_out_g,
           attn_out_g, w_out, norm2_g, w_ff1, w_ff2):
    bsz, seq, d = x.shape
    assert seq % IN_ROW_TILE == 0 and seq % ROW_TILE == 0 and seq % Q_TILE == 0
    assert Q_TILE % ATT_TILE == 0 and (seq // ATT_TILE) % PREP_TILES == 0
    assert w_in.shape[1] == 3 * CONV_DIM + 3 * SB_DIM
    x2 = x.reshape(bsz * seq, d)
    mod3 = _adaln(c, w_ada, b_ada).reshape(bsz, N_MOD, d)
    mix_a, qkv = _in_proj(x2, mod3, norm1_g, w_in, conv_w, conv_out_g, seq)
    y_attn, w_out_b, w1_b, w2_b = _sb_attn(qkv, q_norm_g, k_norm_g, bsz, seq,
                                           (w_out, w_ff1, w_ff2))
    out = _mix_ffn(mix_a, y_attn, x2, mod3, attn_out_g, w_out_b, norm2_g, w1_b, w2_b, seq)
    return out.reshape(bsz, seq, d)
```

```python
import functools

import jax
import jax.numpy as jnp
import numpy as np
from jax import lax
from jax.experimental import pallas as pl
from jax.experimental.pallas import tpu as pltpu

EPS = 1e-6
CONV_DIM = 512
SB_DIM = 512
SB_HEAD_DIM = 64
N_MOD = 6
LANES = 128
HALO = 8
ATT_TILE = 128
Q_TILE = 512
PREP_TILES = 16
LOG2E = 1.4426950408889634
ADALN_COLS = 1024
ROW_TILE = 512
IN_ROW_TILE = 1024
IN_SUB_ROWS = 256
FFN_CHUNK = 512
VMEM_LIMIT = 56 * 1024 * 1024

F32 = jnp.float32
BF16 = jnp.bfloat16


def _split_bf16(a):
    hi = a.astype(BF16)
    lo = (a - hi.astype(F32)).astype(BF16)
    return hi, lo


def _const_spec(shape):
    return pl.BlockSpec(shape, lambda *_: (0,) * len(shape), pipeline_mode=pl.Buffered(1))


def _adaln_kernel(c_ref, w_ref, b_ref, o_ref):
    c = c_ref[...]
    s = c * (1.0 / (1.0 + jnp.exp(-c)))
    s_hi, s_lo = _split_bf16(s)
    w_hi, w_lo = _split_bf16(w_ref[...])
    bsz = c.shape[0]
    dot = functools.partial(jnp.dot, preferred_element_type=F32)
    both = dot(jnp.concatenate([s_hi, s_lo], axis=0), w_hi)
    o_ref[...] = both[:bsz] + (dot(s_hi, w_lo) + both[bsz:]) + b_ref[...]


def _adaln(c, w_ada, b_ada):
    bsz, d = c.shape
    n = w_ada.shape[1]
    tn = ADALN_COLS
    return pl.pallas_call(
        _adaln_kernel,
        grid=(n // tn,),
        in_specs=[pl.BlockSpec((bsz, d), lambda j: (0, 0)),
                  pl.BlockSpec((d, tn), lambda j: (0, j)),
                  pl.BlockSpec((1, tn), lambda j: (0, j))],
        out_specs=pl.BlockSpec((bsz, tn), lambda j: (0, j)),
        out_shape=jax.ShapeDtypeStruct((bsz, n), F32),
        compiler_params=pltpu.CompilerParams(dimension_semantics=("parallel",),
                                             vmem_limit_bytes=VMEM_LIMIT),
        name="adaln",
    )(c, w_ada, b_ada.reshape(1, n))


def _rms(xf, g):
    ms = jnp.mean(xf * xf, axis=-1, keepdims=True)
    return xf * lax.rsqrt(ms + EPS) * g


def _in_proj_kernel(tiles_per_seq, x_ref, mod_ref, g1_ref, w32_ref, cw_ref, cg_ref,
                    mix_ref, qkv_ref, ext_ref, hb_ref, p_ref, w_ref):
    tm = x_ref.shape[0]
    i = pl.program_id(0)

    @pl.when(i == 0)
    def _():
        w_ref[...] = w32_ref[...].astype(BF16)

    @pl.when(i % tiles_per_seq == 0)
    def _():
        ext_ref[0:HALO, :] = jnp.zeros((HALO, CONV_DIM), F32)

    @pl.when(i % tiles_per_seq != 0)
    def _():
        ext_ref[0:HALO, :] = ext_ref[tm:tm + HALO, :]

    shift = mod_ref[0:1, :]
    gain = g1_ref[...] * (1.0 + mod_ref[1:2, :])
    n_conv = 3 * CONV_DIM
    n_sub = tm // IN_SUB_ROWS

    def rows(k):
        return slice(k * IN_SUB_ROWS, (k + 1) * IN_SUB_ROWS)

    def norm(k):
        hb_ref[rows(k), :] = (_rms(x_ref[rows(k), :], gain) + shift).astype(BF16)

    def project(k):
        hb = hb_ref[rows(k), :]
        p_ref[k % 2] = jnp.dot(hb, w_ref[:, :n_conv], preferred_element_type=F32)
        qkv_ref[rows(k), :] = jnp.dot(hb, w_ref[:, n_conv:], preferred_element_type=F32)

    def conv_mixer(k):
        r0 = k * IN_SUB_ROWS
        r1 = r0 + IN_SUB_ROWS
        b_gate = p_ref[k % 2, :, :CONV_DIM]
        cu = p_ref[k % 2, :, CONV_DIM:2 * CONV_DIM] * p_ref[k % 2, :, 2 * CONV_DIM:]
        ext_ref[r0 + HALO:r1 + HALO, :] = cu
        conv = (cw_ref[2:3, :] * cu + cw_ref[1:2, :] * ext_ref[r0 + HALO - 1:r1 + HALO - 1, :]
                + cw_ref[0:1, :] * ext_ref[r0 + HALO - 2:r1 + HALO - 2, :])
        mix_ref[rows(k), :] = _rms(b_gate * conv, cg_ref[...]).astype(BF16)

    norm(0)
    for k in range(n_sub):
        if k + 1 < n_sub:
            norm(k + 1)
        if k >= 1:
            conv_mixer(k - 1)
        project(k)
    conv_mixer(n_sub - 1)


def _in_proj(x2, mod3, norm1_g, w_in, conv_w, conv_out_g, seq):
    rows, d = x2.shape
    tm = IN_ROW_TILE
    tiles_per_seq = seq // tm
    n_in = w_in.shape[1]
    n_qkv = n_in - 3 * CONV_DIM
    return pl.pallas_call(
        functools.partial(_in_proj_kernel, tiles_per_seq),
        grid=(rows // tm,),
        in_specs=[pl.BlockSpec((tm, d), lambda i: (i, 0)),
                  pl.BlockSpec((None, N_MOD, d), lambda i: (i // tiles_per_seq, 0, 0)),
                  _const_spec((1, d)),
                  _const_spec((d, n_in)),
                  _const_spec(conv_w.shape),
                  _const_spec((1, CONV_DIM))],
        out_specs=[pl.BlockSpec((tm, CONV_DIM), lambda i: (i, 0)),
                   pl.BlockSpec((tm, n_qkv), lambda i: (i, 0))],
        out_shape=[jax.ShapeDtypeStruct((rows, CONV_DIM), BF16),
                   jax.ShapeDtypeStruct((rows, n_qkv), F32)],
        scratch_shapes=[pltpu.VMEM((tm + HALO, CONV_DIM), F32),
                        pltpu.VMEM((tm, d), BF16),
                        pltpu.VMEM((2, IN_SUB_ROWS, 3 * CONV_DIM), F32),
                        pltpu.VMEM((d, n_in), BF16)],
        compiler_params=pltpu.CompilerParams(dimension_semantics=("arbitrary",),
                                             vmem_limit_bytes=VMEM_LIMIT),
        name="in_proj",
    )(x2, mod3, norm1_g.reshape(1, d), w_in, conv_w, conv_out_g.reshape(1, CONV_DIM))


def _head_rms(a, g, hsum):
    s_hi, s_lo = _split_bf16(a * a)
    ssq = jnp.dot(jnp.concatenate([s_hi, s_lo], axis=1), hsum, preferred_element_type=F32)
    return a * lax.rsqrt(ssq * (1.0 / SB_HEAD_DIM) + EPS) * g


def _sb_attn_kernel(n_cast, q_ref, k_ref, v_ref, qg_ref, kg_ref, tri_ref, hsum_ref, *refs):
    w32_refs, o_ref, w16_refs = refs[:n_cast], refs[n_cast], refs[n_cast + 1:2 * n_cast + 1]
    (kbd_ref, vbd_ref, qn_ref, acc_ref, carry_ref,
     z_ref, lb_ref, lst_ref, pre_ref, tot_ref) = refs[2 * n_cast + 1:]
    for w32_ref, w16_ref in zip(w32_refs, w16_refs):
        w16_ref[...] = w32_ref[...].astype(BF16)

    t = ATT_TILE
    sub = Q_TILE // t
    n_ktiles = k_ref.shape[0] // t
    n_qtiles = q_ref.shape[0] // Q_TILE
    lo_lane = lax.broadcasted_iota(jnp.int32, (t, LANES), 1) < SB_HEAD_DIM
    lo_row = lax.broadcasted_iota(jnp.int32, (LANES, t), 0) < SB_HEAD_DIM
    row = lax.broadcasted_iota(jnp.int32, (t, 2 * t), 0)
    col = lax.broadcasted_iota(jnp.int32, (t, 2 * t), 1) & (t - 1)
    causal = col < row

    def prep(c, _):
        for i in range(PREP_TILES):
            j = c * PREP_TILES + i
            r0 = pl.multiple_of(j * t, t)
            kt = _head_rms(k_ref[pl.ds(r0, t), :], kg_ref[...], hsum_ref[...]).T
            kbd_ref[j, :, :t] = jnp.where(lo_row, kt, 0.0).astype(BF16)
            kbd_ref[j, :, t:] = jnp.where(lo_row, 0.0, kt).astype(BF16)
            v = v_ref[pl.ds(r0, t), :]
            vbd_ref[j, :t, :] = jnp.where(lo_lane, v, 0.0).astype(BF16)
            vbd_ref[j, t:, :] = jnp.where(lo_lane, 0.0, v).astype(BF16)
            qn = _head_rms(q_ref[pl.ds(r0, t), :], qg_ref[...], hsum_ref[...])
            qn_ref[pl.ds(r0, t), :] = qn.astype(BF16)
        return 0

    lax.fori_loop(0, n_ktiles // PREP_TILES, prep, 0)


    def qk(step):
        j, d, _, q0, slot = step
        q = qn_ref[pl.ds(q0 + d * t, Q_TILE - d * t), :]
        z_ref[slot, d * t:, :] = jnp.dot(q, kbd_ref[j], preferred_element_type=F32)

    def scores(step):
        _, d, diag, _, slot = step
        r0 = d * t
        n = Q_TILE - r0
        z = z_ref[slot, r0:, :]
        sp = jnp.log2(1.0 + jnp.exp2(-jnp.abs(z)))
        log_beta = jnp.minimum(z, 0.0) - sp
        log_1mb = log_beta - z
        if diag:
            masked = jnp.where(causal, log_1mb[:t], 0.0)
            log_1mb = masked if n == t else jnp.concatenate([masked, log_1mb[t:]], axis=0)
        lb_ref[slot, r0:, :] = log_beta
        l16 = log_1mb.astype(BF16)
        lst_ref[slot, 0, r0:, :] = l16[:, :t]
        lst_ref[slot, 1, r0:, :] = l16[:, t:]

    def cumsum(step):
        r0 = step[1] * t
        slot = step[4]
        for h in range(2):
            c = h * t
            cum = jnp.dot(lst_ref[slot, h, r0:, :], tri_ref[...], preferred_element_type=F32)
            pre_ref[slot, r0:, c:c + t] = lb_ref[slot, r0:, c:c + t] + cum[:, :t]
            tot_ref[slot, r0:, c:c + t] = cum[:, t:]

    def weights(step):
        j, d, diag, _, slot = step
        r0 = d * t
        n = Q_TILE - r0
        pre_s, tot_s = pre_ref.at[slot], tot_ref.at[slot]
        if diag:
            a = jnp.where(causal, jnp.exp2(pre_s[r0:r0 + t, :]), 0.0)
            carry_ref[r0:r0 + t, :] = tot_s[r0:r0 + t, :]
            if n > t:
                carry = carry_ref[r0 + t:, :]
                a = jnp.concatenate([a, jnp.exp2(pre_s[r0 + t:, :] + carry)], axis=0)
                carry_ref[r0 + t:, :] = carry + tot_s[r0 + t:, :]
        else:
            carry = carry_ref[...]
            a = jnp.exp2(pre_s[...] + carry)
            carry_ref[...] = carry + tot_s[...]
        av = jnp.dot(a.astype(BF16), vbd_ref[j], preferred_element_type=F32)
        if diag:
            acc_ref[r0:r0 + t, :] = av[:t]
            if n > t:
                acc_ref[r0 + t:, :] += av[t:]
        else:
            acc_ref[...] += av

    stages = (qk, scores, cumsum, weights)
    n_stages = len(stages)

    steps = []
    for qi in range(n_qtiles):
        jd = qi * sub
        tile_steps = [(jd + d, d, True, qi * Q_TILE) for d in reversed(range(sub))]
        tile_steps += [(j, 0, False, qi * Q_TILE) for j in reversed(range(jd))]
        steps += [(st + ((len(steps) + i) % 2,), i == len(tile_steps) - 1)
                  for i, st in enumerate(tile_steps)]
    for g in range(len(steps) + n_stages - 1):
        for s in reversed(range(n_stages)):
            if 0 <= g - s < len(steps):
                st, last_of_tile = steps[g - s]
                stages[s](st)
                if last_of_tile and s == n_stages - 1:
                    o_ref[st[3]:st[3] + Q_TILE, :] = acc_ref[...]


def _sb_attn(qkv, q_norm_g, k_norm_g, bsz, seq, cast_weights):
    rows = qkv.shape[0]
    t = ATT_TILE
    n_pairs = SB_DIM // LANES
    n_tiles = seq // t
    n_steps = bsz * n_pairs
    slabs = [w.shape[0] // n_steps for w in cast_weights]
    assert all(w.shape[0] == s * n_steps and s % 16 == 0 for w, s in zip(cast_weights, slabs))
    slab_specs = [pl.BlockSpec((s, w.shape[1]), lambda b, p: (b * n_pairs + p, 0))
                  for w, s in zip(cast_weights, slabs)]
    j_idx = np.arange(t)[:, None]
    s_idx = np.arange(2 * t)[None, :]
    tri = jnp.asarray((s_idx >= t) | (j_idx > s_idx), BF16)
    lane_head = np.arange(LANES) // SB_HEAD_DIM
    hsum = jnp.asarray(np.tile(lane_head, 2)[:, None] == lane_head[None, :], BF16)
    reps = LANES // SB_HEAD_DIM
    qg = (jnp.tile(q_norm_g, reps) * (SB_HEAD_DIM ** -0.5 * LOG2E)).reshape(1, LANES)
    kg = jnp.tile(k_norm_g, reps).reshape(1, LANES)
    return pl.pallas_call(
        functools.partial(_sb_attn_kernel, len(cast_weights)),
        grid=(bsz, n_pairs),
        in_specs=[pl.BlockSpec((seq, LANES), lambda b, p: (b, p)),
                  pl.BlockSpec((seq, LANES), lambda b, p: (b, n_pairs + p)),
                  pl.BlockSpec((seq, LANES), lambda b, p: (b, 2 * n_pairs + p)),
                  _const_spec((1, LANES)),
                  _const_spec((1, LANES)),
                  _const_spec((t, 2 * t)),
                  _const_spec((2 * LANES, LANES))] + slab_specs,
        out_specs=[pl.BlockSpec((seq, LANES), lambda b, p: (b, p))] + slab_specs,
        out_shape=[jax.ShapeDtypeStruct((rows, SB_DIM), F32)]
        + [jax.ShapeDtypeStruct(w.shape, BF16) for w in cast_weights],
        scratch_shapes=[pltpu.VMEM((n_tiles, LANES, 2 * t), BF16),
                        pltpu.VMEM((n_tiles, 2 * t, LANES), BF16),
                        pltpu.VMEM((seq, LANES), BF16),
                        pltpu.VMEM((Q_TILE, LANES), F32),
                        pltpu.VMEM((Q_TILE, 2 * t), F32),
                        pltpu.VMEM((2, Q_TILE, 2 * t), F32),
                        pltpu.VMEM((2, Q_TILE, 2 * t), F32),
                        pltpu.VMEM((2, 2, Q_TILE, t), BF16),
                        pltpu.VMEM((2, Q_TILE, 2 * t), F32),
                        pltpu.VMEM((2, Q_TILE, 2 * t), F32)],
        compiler_params=pltpu.CompilerParams(dimension_semantics=("parallel", "parallel"),
                                             vmem_limit_bytes=VMEM_LIMIT),
        name="sb_attn",
    )(qkv, qkv, qkv, qg, kg, tri, hsum, *cast_weights)


def _mix_ffn_kernel(mix_ref, ya_ref, x_ref, mod_ref, ag_ref, wo_ref, g2_ref, w1_ref, w2_ref,
                    o_ref):
    d_ff = w1_ref.shape[1]
    yb = _rms(ya_ref[...], ag_ref[...]).astype(BF16)
    o = (jnp.dot(mix_ref[...], wo_ref[:CONV_DIM, :], preferred_element_type=F32)
         + jnp.dot(yb, wo_ref[CONV_DIM:, :], preferred_element_type=F32))
    x1 = x_ref[...] + mod_ref[2:3, :] * o
    gain2 = g2_ref[...] * (1.0 + mod_ref[4:5, :])
    h2 = (_rms(x1, gain2) + mod_ref[3:4, :]).astype(BF16)
    f = None
    for c0 in range(0, d_ff, FFN_CHUNK):
        u = jnp.dot(h2, w1_ref[:, c0:c0 + FFN_CHUNK], preferred_element_type=F32)
        r = jnp.maximum(u, 0.0)
        fc = jnp.dot((r * r).astype(BF16), w2_ref[c0:c0 + FFN_CHUNK, :],
                     preferred_element_type=F32)
        f = fc if f is None else f + fc
    o_ref[...] = x1 + mod_ref[5:6, :] * f


def _mix_ffn(mix_a, y_attn, x2, mod3, attn_out_g, w_out_b, norm2_g, w1_b, w2_b, seq):
    rows, d = x2.shape
    tm = ROW_TILE
    tiles_per_seq = seq // tm
    return pl.pallas_call(
        _mix_ffn_kernel,
        grid=(rows // tm,),
        in_specs=[pl.BlockSpec((tm, CONV_DIM), lambda i: (i, 0)),
                  pl.BlockSpec((tm, SB_DIM), lambda i: (i, 0)),
                  pl.BlockSpec((tm, d), lambda i: (i, 0)),
                  pl.BlockSpec((None, N_MOD, d), lambda i: (i // tiles_per_seq, 0, 0)),
                  _const_spec((1, SB_DIM)),
                  _const_spec(w_out_b.shape),
                  _const_spec((1, d)),
                  _const_spec(w1_b.shape),
                  _const_spec(w2_b.shape)],
        out_specs=pl.BlockSpec((tm, d), lambda i: (i, 0)),
        out_shape=jax.ShapeDtypeStruct((rows, d), F32),
        compiler_params=pltpu.CompilerParams(dimension_semantics=("parallel",),
                                             vmem_limit_bytes=VMEM_LIMIT),
        name="mix_ffn",
    )(mix_a, y_attn, x2, mod3, attn_out_g.reshape(1, SB_DIM), w_out_b, norm2_g.reshape(1, d),
      w1_b, w2_b)


def kernel(x, c, w_ada, b_ada, norm1_g, w_in, conv_w, q_norm_g, k_norm_g, conv_out_g,
           attn_out_g, w_out, norm2_g, w_ff1, w_ff2):
    bsz, seq, d = x.shape
    assert seq % IN_ROW_TILE == 0 and seq % ROW_TILE == 0 and seq % Q_TILE == 0
    assert Q_TILE % ATT_TILE == 0 and (seq // ATT_TILE) % PREP_TILES == 0
    assert w_in.shape[1] == 3 * CONV_DIM + 3 * SB_DIM
    x2 = x.reshape(bsz * seq, d)
    mod3 = _adaln(c, w_ada, b_ada).reshape(bsz, N_MOD, d)
    mix_a, qkv = _in_proj(x2, mod3, norm1_g, w_in, conv_w, conv_out_g, seq)
    y_attn, w_out_b, w1_b, w2_b = _sb_attn(qkv, q_norm_g, k_norm_g, bsz, seq,
                                           (w_out, w_ff1, w_ff2))
    out = _mix_ffn(mix_a, y_attn, x2, mod3, attn_out_g, w_out_b, norm2_g, w1_b, w2_b, seq)
    return out.reshape(bsz, seq, d)
```

```python
import functools

import jax
import jax.numpy as jnp
import numpy as np
from jax import lax
from jax.experimental import pallas as pl
from jax.experimental.pallas import tpu as pltpu

EPS = 1e-6
CONV_DIM = 512
SB_DIM = 512
SB_HEAD_DIM = 64
N_MOD = 6
LANES = 128
HALO = 8
ATT_TILE = 128
Q_TILE = 512
PAIRS_PER_STEP = 2
LOG2E = 1.4426950408889634
ADALN_COLS = 1024
ROW_TILE = 512
IN_ROW_TILE = 1024
IN_SUB_ROWS = 256
FFN_CHUNK = 512
VMEM_LIMIT = 56 * 1024 * 1024

F32 = jnp.float32
BF16 = jnp.bfloat16


def _split_bf16(a):
    hi = a.astype(BF16)
    lo = (a - hi.astype(F32)).astype(BF16)
    return hi, lo


def _const_spec(shape):
    return pl.BlockSpec(shape, lambda *_: (0,) * len(shape), pipeline_mode=pl.Buffered(1))


def _adaln_kernel(c_ref, w_ref, b_ref, o_ref):
    c = c_ref[...]
    s = c * (1.0 / (1.0 + jnp.exp(-c)))
    s_hi, s_lo = _split_bf16(s)
    w_hi, w_lo = _split_bf16(w_ref[...])
    bsz = c.shape[0]
    dot = functools.partial(jnp.dot, preferred_element_type=F32)
    both = dot(jnp.concatenate([s_hi, s_lo], axis=0), w_hi)
    o_ref[...] = both[:bsz] + (dot(s_hi, w_lo) + both[bsz:]) + b_ref[...]


def _adaln(c, w_ada, b_ada):
    bsz, d = c.shape
    n = w_ada.shape[1]
    tn = ADALN_COLS
    return pl.pallas_call(
        _adaln_kernel,
        grid=(n // tn,),
        in_specs=[pl.BlockSpec((bsz, d), lambda j: (0, 0)),
                  pl.BlockSpec((d, tn), lambda j: (0, j)),
                  pl.BlockSpec((1, tn), lambda j: (0, j))],
        out_specs=pl.BlockSpec((bsz, tn), lambda j: (0, j)),
        out_shape=jax.ShapeDtypeStruct((bsz, n), F32),
        compiler_params=pltpu.CompilerParams(dimension_semantics=("parallel",),
                                             vmem_limit_bytes=VMEM_LIMIT),
        name="adaln",
    )(c, w_ada, b_ada.reshape(1, n))


def _rms(xf, g):
    ms = jnp.mean(xf * xf, axis=-1, keepdims=True)
    return xf * lax.rsqrt(ms + EPS) * g


def _in_proj_kernel(tiles_per_seq, x_ref, mod_ref, g1_ref, w32_ref, cw_ref, cg_ref,
                    mix_ref, qkv_ref, ext_ref, hb_ref, p_ref, w_ref):
    tm = x_ref.shape[0]
    i = pl.program_id(0)

    @pl.when(i == 0)
    def _():
        w_ref[...] = w32_ref[...].astype(BF16)

    @pl.when(i % tiles_per_seq == 0)
    def _():
        ext_ref[0:HALO, :] = jnp.zeros((HALO, CONV_DIM), F32)

    @pl.when(i % tiles_per_seq != 0)
    def _():
        ext_ref[0:HALO, :] = ext_ref[tm:tm + HALO, :]

    shift = mod_ref[0:1, :]
    gain = g1_ref[...] * (1.0 + mod_ref[1:2, :])
    n_conv = 3 * CONV_DIM
    n_sub = tm // IN_SUB_ROWS

    def rows(k):
        return slice(k * IN_SUB_ROWS, (k + 1) * IN_SUB_ROWS)

    def norm(k):
        hb_ref[rows(k), :] = (_rms(x_ref[rows(k), :], gain) + shift).astype(BF16)

    def project(k):
        hb = hb_ref[rows(k), :]
        p_ref[k % 2] = jnp.dot(hb, w_ref[:, :n_conv], preferred_element_type=F32)
        qkv_ref[rows(k), :] = jnp.dot(hb, w_ref[:, n_conv:], preferred_element_type=F32)

    def conv_mixer(k):
        r0 = k * IN_SUB_ROWS
        r1 = r0 + IN_SUB_ROWS
        b_gate = p_ref[k % 2, :, :CONV_DIM]
        cu = p_ref[k % 2, :, CONV_DIM:2 * CONV_DIM] * p_ref[k % 2, :, 2 * CONV_DIM:]
        ext_ref[r0 + HALO:r1 + HALO, :] = cu
        conv = (cw_ref[2:3, :] * cu + cw_ref[1:2, :] * ext_ref[r0 + HALO - 1:r1 + HALO - 1, :]
                + cw_ref[0:1, :] * ext_ref[r0 + HALO - 2:r1 + HALO - 2, :])
        mix_ref[rows(k), :] = _rms(b_gate * conv, cg_ref[...]).astype(BF16)

    norm(0)
    for k in range(n_sub):
        if k + 1 < n_sub:
            norm(k + 1)
        if k >= 1:
            conv_mixer(k - 1)
        project(k)
    conv_mixer(n_sub - 1)


def _in_proj(x2, mod3, norm1_g, w_in, conv_w, conv_out_g, seq):
    rows, d = x2.shape
    tm = IN_ROW_TILE
    tiles_per_seq = seq // tm
    n_in = w_in.shape[1]
    n_qkv = n_in - 3 * CONV_DIM
    return pl.pallas_call(
        functools.partial(_in_proj_kernel, tiles_per_seq),
        grid=(rows // tm,),
        in_specs=[pl.BlockSpec((tm, d), lambda i: (i, 0)),
                  pl.BlockSpec((None, N_MOD, d), lambda i: (i // tiles_per_seq, 0, 0)),
                  _const_spec((1, d)),
                  _const_spec((d, n_in)),
                  _const_spec(conv_w.shape),
                  _const_spec((1, CONV_DIM))],
        out_specs=[pl.BlockSpec((tm, CONV_DIM), lambda i: (i, 0)),
                   pl.BlockSpec((tm, n_qkv), lambda i: (i, 0))],
        out_shape=[jax.ShapeDtypeStruct((rows, CONV_DIM), BF16),
                   jax.ShapeDtypeStruct((rows, n_qkv), F32)],
        scratch_shapes=[pltpu.VMEM((tm + HALO, CONV_DIM), F32),
                        pltpu.VMEM((tm, d), BF16),
                        pltpu.VMEM((2, IN_SUB_ROWS, 3 * CONV_DIM), F32),
                        pltpu.VMEM((d, n_in), BF16)],
        compiler_params=pltpu.CompilerParams(dimension_semantics=("arbitrary",),
                                             vmem_limit_bytes=VMEM_LIMIT),
        name="in_proj",
    )(x2, mod3, norm1_g.reshape(1, d), w_in, conv_w, conv_out_g.reshape(1, CONV_DIM))


def _head_rms(a, g, hsum):
    s_hi, s_lo = _split_bf16(a * a)
    ssq = jnp.dot(jnp.concatenate([s_hi, s_lo], axis=1), hsum, preferred_element_type=F32)
    return a * lax.rsqrt(ssq * (1.0 / SB_HEAD_DIM) + EPS) * g


def _sb_attn_kernel(n_cast, q_ref, k_ref, v_ref, qg_ref, kg_ref, tri_ref, hsum_ref, *refs):
    w32_refs, o_ref, w16_refs = refs[:n_cast], refs[n_cast], refs[n_cast + 1:2 * n_cast + 1]
    (kbd_ref, vbd_ref, qn_ref, acc_ref, carry_ref,
     z_ref, lb_ref, lst_ref, pre_ref, tot_ref) = refs[2 * n_cast + 1:]
    for w32_ref, w16_ref in zip(w32_refs, w16_refs):
        w16_ref[...] = w32_ref[...].astype(BF16)

    t = ATT_TILE
    sub = Q_TILE // t
    n_ktiles = k_ref.shape[0] // t
    n_qtiles = q_ref.shape[0] // Q_TILE
    lo_lane = lax.broadcasted_iota(jnp.int32, (t, LANES), 1) < SB_HEAD_DIM
    lo_row = lax.broadcasted_iota(jnp.int32, (LANES, t), 0) < SB_HEAD_DIM
    row = lax.broadcasted_iota(jnp.int32, (t, 2 * t), 0)
    col = lax.broadcasted_iota(jnp.int32, (t, 2 * t), 1) & (t - 1)
    causal = col < row

    def prep(p):
        lanes = slice(p * LANES, (p + 1) * LANES)
        for j in range(n_ktiles):
            r0 = j * t
            kt = _head_rms(k_ref[r0:r0 + t, lanes], kg_ref[...], hsum_ref[...]).T
            kbd_ref[p, j, :, :t] = jnp.where(lo_row, kt, 0.0).astype(BF16)
            kbd_ref[p, j, :, t:] = jnp.where(lo_row, 0.0, kt).astype(BF16)
            v = v_ref[r0:r0 + t, lanes]
            vbd_ref[p, j, :t, :] = jnp.where(lo_lane, v, 0.0).astype(BF16)
            vbd_ref[p, j, t:, :] = jnp.where(lo_lane, 0.0, v).astype(BF16)
            qn = _head_rms(q_ref[r0:r0 + t, lanes], qg_ref[...], hsum_ref[...])
            qn_ref[p, r0:r0 + t, :] = qn.astype(BF16)


    def qk(step):
        j, d, _, q0, slot, p = step
        q = qn_ref[p, q0 + d * t:q0 + Q_TILE, :]
        z_ref[slot, d * t:, :] = jnp.dot(q, kbd_ref[p, j], preferred_element_type=F32)

    def scores(step):
        _, d, diag, _, slot, _ = step
        r0 = d * t
        n = Q_TILE - r0
        z = z_ref[slot, r0:, :]
        sp = jnp.log2(1.0 + jnp.exp2(-jnp.abs(z)))
        log_beta = jnp.minimum(z, 0.0) - sp
        log_1mb = log_beta - z
        if diag:
            masked = jnp.where(causal, log_1mb[:t], 0.0)
            log_1mb = masked if n == t else jnp.concatenate([masked, log_1mb[t:]], axis=0)
        lb_ref[slot, r0:, :] = log_beta
        l16 = log_1mb.astype(BF16)
        lst_ref[slot, 0, r0:, :] = l16[:, :t]
        lst_ref[slot, 1, r0:, :] = l16[:, t:]

    def cumsum(step):
        r0 = step[1] * t
        slot = step[4]
        for h in range(2):
            c = h * t
            cum = jnp.dot(lst_ref[slot, h, r0:, :], tri_ref[...], preferred_element_type=F32)
            pre_ref[slot, r0:, c:c + t] = lb_ref[slot, r0:, c:c + t] + cum[:, :t]
            tot_ref[slot, r0:, c:c + t] = cum[:, t:]

    def weights(step):
        j, d, diag, _, slot, p = step
        r0 = d * t
        n = Q_TILE - r0
        pre_s, tot_s = pre_ref.at[slot], tot_ref.at[slot]
        if diag:
            a = jnp.where(causal, jnp.exp2(pre_s[r0:r0 + t, :]), 0.0)
            carry_ref[r0:r0 + t, :] = tot_s[r0:r0 + t, :]
            if n > t:
                carry = carry_ref[r0 + t:, :]
                a = jnp.concatenate([a, jnp.exp2(pre_s[r0 + t:, :] + carry)], axis=0)
                carry_ref[r0 + t:, :] = carry + tot_s[r0 + t:, :]
        else:
            carry = carry_ref[...]
            a = jnp.exp2(pre_s[...] + carry)
            carry_ref[...] = carry + tot_s[...]
        av = jnp.dot(a.astype(BF16), vbd_ref[p, j], preferred_element_type=F32)
        if diag:
            acc_ref[r0:r0 + t, :] = av[:t]
            if n > t:
                acc_ref[r0 + t:, :] += av[t:]
        else:
            acc_ref[...] += av

    stages = (qk, scores, cumsum, weights)
    n_stages = len(stages)

    steps = []
    for p in range(PAIRS_PER_STEP):
        for qi in range(n_qtiles):
            jd = qi * sub
            tile_steps = [(jd + d, d, True, qi * Q_TILE) for d in reversed(range(sub))]
            tile_steps += [(j, 0, False, qi * Q_TILE) for j in reversed(range(jd))]
            steps += [(st + ((len(steps) + i) % 2, p), i == len(tile_steps) - 1)
                      for i, st in enumerate(tile_steps)]
    for p in range(PAIRS_PER_STEP):
        prep(p)
    for g in range(len(steps) + n_stages - 1):
        for s in reversed(range(n_stages)):
            if 0 <= g - s < len(steps):
                st, last_of_tile = steps[g - s]
                stages[s](st)
                if last_of_tile and s == n_stages - 1:
                    q0, p = st[3], st[5]
                    o_ref[q0:q0 + Q_TILE, p * LANES:(p + 1) * LANES] = acc_ref[...]


def _sb_attn(qkv, q_norm_g, k_norm_g, bsz, seq, cast_weights):
    rows = qkv.shape[0]
    t = ATT_TILE
    n_groups = SB_DIM // (LANES * PAIRS_PER_STEP)
    n_tiles = seq // t
    n_steps = bsz * n_groups
    width = LANES * PAIRS_PER_STEP
    slabs = [w.shape[0] // n_steps for w in cast_weights]
    assert all(w.shape[0] == s * n_steps and s % 16 == 0 for w, s in zip(cast_weights, slabs))
    slab_specs = [pl.BlockSpec((s, w.shape[1]), lambda b, p: (b * n_groups + p, 0))
                  for w, s in zip(cast_weights, slabs)]
    j_idx = np.arange(t)[:, None]
    s_idx = np.arange(2 * t)[None, :]
    tri = jnp.asarray((s_idx >= t) | (j_idx > s_idx), BF16)
    lane_head = np.arange(LANES) // SB_HEAD_DIM
    hsum = jnp.asarray(np.tile(lane_head, 2)[:, None] == lane_head[None, :], BF16)
    reps = LANES // SB_HEAD_DIM
    qg = (jnp.tile(q_norm_g, reps) * (SB_HEAD_DIM ** -0.5 * LOG2E)).reshape(1, LANES)
    kg = jnp.tile(k_norm_g, reps).reshape(1, LANES)
    return pl.pallas_call(
        functools.partial(_sb_attn_kernel, len(cast_weights)),
        grid=(bsz, n_groups),
        in_specs=[pl.BlockSpec((seq, width), lambda b, p: (b, p)),
                  pl.BlockSpec((seq, width), lambda b, p: (b, n_groups + p)),
                  pl.BlockSpec((seq, width), lambda b, p: (b, 2 * n_groups + p)),
                  _const_spec((1, LANES)),
                  _const_spec((1, LANES)),
                  _const_spec((t, 2 * t)),
                  _const_spec((2 * LANES, LANES))] + slab_specs,
        out_specs=[pl.BlockSpec((seq, width), lambda b, p: (b, p))] + slab_specs,
        out_shape=[jax.ShapeDtypeStruct((rows, SB_DIM), F32)]
        + [jax.ShapeDtypeStruct(w.shape, BF16) for w in cast_weights],
        scratch_shapes=[pltpu.VMEM((PAIRS_PER_STEP, n_tiles, LANES, 2 * t), BF16),
                        pltpu.VMEM((PAIRS_PER_STEP, n_tiles, 2 * t, LANES), BF16),
                        pltpu.VMEM((PAIRS_PER_STEP, seq, LANES), BF16),
                        pltpu.VMEM((Q_TILE, LANES), F32),
                        pltpu.VMEM((Q_TILE, 2 * t), F32),
                        pltpu.VMEM((2, Q_TILE, 2 * t), F32),
                        pltpu.VMEM((2, Q_TILE, 2 * t), F32),
                        pltpu.VMEM((2, 2, Q_TILE, t), BF16),
                        pltpu.VMEM((2, Q_TILE, 2 * t), F32),
                        pltpu.VMEM((2, Q_TILE, 2 * t), F32)],
        compiler_params=pltpu.CompilerParams(dimension_semantics=("parallel", "parallel"),
                                             vmem_limit_bytes=VMEM_LIMIT),
        name="sb_attn",
    )(qkv, qkv, qkv, qg, kg, tri, hsum, *cast_weights)


def _mix_ffn_kernel(mix_ref, ya_ref, x_ref, mod_ref, ag_ref, wo_ref, g2_ref, w1_ref, w2_ref,
                    o_ref):
    d_ff = w1_ref.shape[1]
    yb = _rms(ya_ref[...], ag_ref[...]).astype(BF16)
    o = (jnp.dot(mix_ref[...], wo_ref[:CONV_DIM, :], preferred_element_type=F32)
         + jnp.dot(yb, wo_ref[CONV_DIM:, :], preferred_element_type=F32))
    x1 = x_ref[...] + mod_ref[2:3, :] * o
    gain2 = g2_ref[...] * (1.0 + mod_ref[4:5, :])
    h2 = (_rms(x1, gain2) + mod_ref[3:4, :]).astype(BF16)
    f = None
    for c0 in range(0, d_ff, FFN_CHUNK):
        u = jnp.dot(h2, w1_ref[:, c0:c0 + FFN_CHUNK], preferred_element_type=F32)
        r = jnp.maximum(u, 0.0)
        fc = jnp.dot((r * r).astype(BF16), w2_ref[c0:c0 + FFN_CHUNK, :],
                     preferred_element_type=F32)
        f = fc if f is None else f + fc
    o_ref[...] = x1 + mod_ref[5:6, :] * f


def _mix_ffn(mix_a, y_attn, x2, mod3, attn_out_g, w_out_b, norm2_g, w1_b, w2_b, seq):
    rows, d = x2.shape
    tm = ROW_TILE
    tiles_per_seq = seq // tm
    return pl.pallas_call(
        _mix_ffn_kernel,
        grid=(rows // tm,),
        in_specs=[pl.BlockSpec((tm, CONV_DIM), lambda i: (i, 0)),
                  pl.BlockSpec((tm, SB_DIM), lambda i: (i, 0)),
                  pl.BlockSpec((tm, d), lambda i: (i, 0)),
                  pl.BlockSpec((None, N_MOD, d), lambda i: (i // tiles_per_seq, 0, 0)),
                  _const_spec((1, SB_DIM)),
                  _const_spec(w_out_b.shape),
                  _const_spec((1, d)),
                  _const_spec(w1_b.shape),
                  _const_spec(w2_b.shape)],
        out_specs=pl.BlockSpec((tm, d), lambda i: (i, 0)),
        out_shape=jax.ShapeDtypeStruct((rows, d), F32),
        compiler_params=pltpu.CompilerParams(dimension_semantics=("parallel",),
                                             vmem_limit_bytes=VMEM_LIMIT),
        name="mix_ffn",
    )(mix_a, y_attn, x2, mod3, attn_out_g.reshape(1, SB_DIM), w_out_b, norm2_g.reshape(1, d),
      w1_b, w2_b)


def kernel(x, c, w_ada, b_ada, norm1_g, w_in, conv_w, q_norm_g, k_norm_g, conv_out_g,
           attn_out_g, w_out, norm2_g, w_ff1, w_ff2):
    bsz, seq, d = x.shape
    assert seq % IN_ROW_TILE == 0 and seq % ROW_TILE == 0 and seq % Q_TILE == 0
    assert Q_TILE % ATT_TILE == 0 and SB_DIM % (LANES * PAIRS_PER_STEP) == 0
    assert w_in.shape[1] == 3 * CONV_DIM + 3 * SB_DIM
    x2 = x.reshape(bsz * seq, d)
    mod3 = _adaln(c, w_ada, b_ada).reshape(bsz, N_MOD, d)
    mix_a, qkv = _in_proj(x2, mod3, norm1_g, w_in, conv_w, conv_out_g, seq)
    y_attn, w_out_b, w1_b, w2_b = _sb_attn(qkv, q_norm_g, k_norm_g, bsz, seq,
                                           (w_out, w_ff1, w_ff2))
    out = _mix_ffn(mix_a, y_attn, x2, mod3, attn_out_g, w_out_b, norm2_g, w1_b, w2_b, seq)
    return out.reshape(bsz, seq, d)
```

```python
import functools

import jax
import jax.numpy as jnp
import numpy as np
from jax import lax
from jax.experimental import pallas as pl
from jax.experimental.pallas import tpu as pltpu

EPS = 1e-6
CONV_DIM = 512
SB_DIM = 512
SB_HEAD_DIM = 64
N_MOD = 6
LANES = 128
HALO = 8
ATT_TILE = 128
Q_TILE = 512
PREP_TILES = 16
LOG2E = 1.4426950408889634
ADALN_COLS = 1024
ROW_TILE = 1024
IN_ROW_TILE = 1024
IN_SUB_ROWS = 256
FFN_CHUNK = 512
VMEM_LIMIT = 56 * 1024 * 1024

F32 = jnp.float32
BF16 = jnp.bfloat16


def _split_bf16(a):
    hi = a.astype(BF16)
    lo = (a - hi.astype(F32)).astype(BF16)
    return hi, lo


def _const_spec(shape):
    return pl.BlockSpec(shape, lambda *_: (0,) * len(shape), pipeline_mode=pl.Buffered(1))


def _adaln_kernel(c_ref, w_ref, b_ref, o_ref):
    c = c_ref[...]
    s = c * (1.0 / (1.0 + jnp.exp(-c)))
    s_hi, s_lo = _split_bf16(s)
    w_hi, w_lo = _split_bf16(w_ref[...])
    bsz = c.shape[0]
    dot = functools.partial(jnp.dot, preferred_element_type=F32)
    both = dot(jnp.concatenate([s_hi, s_lo], axis=0), w_hi)
    o_ref[...] = both[:bsz] + (dot(s_hi, w_lo) + both[bsz:]) + b_ref[...]


def _adaln(c, w_ada, b_ada):
    bsz, d = c.shape
    n = w_ada.shape[1]
    tn = ADALN_COLS
    return pl.pallas_call(
        _adaln_kernel,
        grid=(n // tn,),
        in_specs=[pl.BlockSpec((bsz, d), lambda j: (0, 0)),
                  pl.BlockSpec((d, tn), lambda j: (0, j)),
                  pl.BlockSpec((1, tn), lambda j: (0, j))],
        out_specs=pl.BlockSpec((bsz, tn), lambda j: (0, j)),
        out_shape=jax.ShapeDtypeStruct((bsz, n), F32),
        compiler_params=pltpu.CompilerParams(dimension_semantics=("parallel",),
                                             vmem_limit_bytes=VMEM_LIMIT),
        name="adaln",
    )(c, w_ada, b_ada.reshape(1, n))


def _rms(xf, g):
    ms = jnp.mean(xf * xf, axis=-1, keepdims=True)
    return xf * lax.rsqrt(ms + EPS) * g


def _in_proj_kernel(tiles_per_seq, x_ref, mod_ref, g1_ref, w32_ref, cw_ref, cg_ref,
                    mix_ref, qkv_ref, ext_ref, hb_ref, p_ref, w_ref):
    tm = x_ref.shape[0]
    i = pl.program_id(0)

    @pl.when(i == 0)
    def _():
        w_ref[...] = w32_ref[...].astype(BF16)

    @pl.when(i % tiles_per_seq == 0)
    def _():
        ext_ref[0:HALO, :] = jnp.zeros((HALO, CONV_DIM), F32)

    @pl.when(i % tiles_per_seq != 0)
    def _():
        ext_ref[0:HALO, :] = ext_ref[tm:tm + HALO, :]

    shift = mod_ref[0:1, :]
    gain = g1_ref[...] * (1.0 + mod_ref[1:2, :])
    n_conv = 3 * CONV_DIM
    n_sub = tm // IN_SUB_ROWS

    def rows(k):
        return slice(k * IN_SUB_ROWS, (k + 1) * IN_SUB_ROWS)

    def norm(k):
        hb_ref[rows(k), :] = (_rms(x_ref[rows(k), :], gain) + shift).astype(BF16)

    def project(k):
        hb = hb_ref[rows(k), :]
        p_ref[k % 2] = jnp.dot(hb, w_ref[:, :n_conv], preferred_element_type=F32)
        qkv_ref[rows(k), :] = jnp.dot(hb, w_ref[:, n_conv:], preferred_element_type=F32)

    def conv_mixer(k):
        r0 = k * IN_SUB_ROWS
        r1 = r0 + IN_SUB_ROWS
        b_gate = p_ref[k % 2, :, :CONV_DIM]
        cu = p_ref[k % 2, :, CONV_DIM:2 * CONV_DIM] * p_ref[k % 2, :, 2 * CONV_DIM:]
        ext_ref[r0 + HALO:r1 + HALO, :] = cu
        conv = (cw_ref[2:3, :] * cu + cw_ref[1:2, :] * ext_ref[r0 + HALO - 1:r1 + HALO - 1, :]
                + cw_ref[0:1, :] * ext_ref[r0 + HALO - 2:r1 + HALO - 2, :])
        mix_ref[rows(k), :] = _rms(b_gate * conv, cg_ref[...]).astype(BF16)

    norm(0)
    for k in range(n_sub):
        if k + 1 < n_sub:
            norm(k + 1)
        if k >= 1:
            conv_mixer(k - 1)
        project(k)
    conv_mixer(n_sub - 1)


def _in_proj(x2, mod3, norm1_g, w_in, conv_w, conv_out_g, seq):
    rows, d = x2.shape
    tm = IN_ROW_TILE
    tiles_per_seq = seq // tm
    n_in = w_in.shape[1]
    n_qkv = n_in - 3 * CONV_DIM
    return pl.pallas_call(
        functools.partial(_in_proj_kernel, tiles_per_seq),
        grid=(rows // tm,),
        in_specs=[pl.BlockSpec((tm, d), lambda i: (i, 0)),
                  pl.BlockSpec((None, N_MOD, d), lambda i: (i // tiles_per_seq, 0, 0)),
                  _const_spec((1, d)),
                  _const_spec((d, n_in)),
                  _const_spec(conv_w.shape),
                  _const_spec((1, CONV_DIM))],
        out_specs=[pl.BlockSpec((tm, CONV_DIM), lambda i: (i, 0)),
                   pl.BlockSpec((tm, n_qkv), lambda i: (i, 0))],
        out_shape=[jax.ShapeDtypeStruct((rows, CONV_DIM), BF16),
                   jax.ShapeDtypeStruct((rows, n_qkv), F32)],
        scratch_shapes=[pltpu.VMEM((tm + HALO, CONV_DIM), F32),
                        pltpu.VMEM((tm, d), BF16),
                        pltpu.VMEM((2, IN_SUB_ROWS, 3 * CONV_DIM), F32),
                        pltpu.VMEM((d, n_in), BF16)],
        compiler_params=pltpu.CompilerParams(dimension_semantics=("arbitrary",),
                                             vmem_limit_bytes=VMEM_LIMIT),
        name="in_proj",
    )(x2, mod3, norm1_g.reshape(1, d), w_in, conv_w, conv_out_g.reshape(1, CONV_DIM))


def _head_rms(a, g, hsum):
    s_hi, s_lo = _split_bf16(a * a)
    ssq = jnp.dot(jnp.concatenate([s_hi, s_lo], axis=1), hsum, preferred_element_type=F32)
    return a * lax.rsqrt(ssq * (1.0 / SB_HEAD_DIM) + EPS) * g


def _sb_attn_kernel(n_cast, q_ref, k_ref, v_ref, qg_ref, kg_ref, tri_ref, hsum_ref, *refs):
    w32_refs, o_ref, w16_refs = refs[:n_cast], refs[n_cast], refs[n_cast + 1:2 * n_cast + 1]
    (kbd_ref, vbd_ref, qn_ref, acc_ref, carry_ref,
     z_ref, lb_ref, lst_ref, pre_ref, tot_ref) = refs[2 * n_cast + 1:]
    for w32_ref, w16_ref in zip(w32_refs, w16_refs):
        w16_ref[...] = w32_ref[...].astype(BF16)

    t = ATT_TILE
    sub = Q_TILE // t
    n_ktiles = k_ref.shape[0] // t
    n_qtiles = q_ref.shape[0] // Q_TILE
    lo_lane = lax.broadcasted_iota(jnp.int32, (t, LANES), 1) < SB_HEAD_DIM
    lo_row = lax.broadcasted_iota(jnp.int32, (LANES, t), 0) < SB_HEAD_DIM
    row = lax.broadcasted_iota(jnp.int32, (t, 2 * t), 0)
    col = lax.broadcasted_iota(jnp.int32, (t, 2 * t), 1) & (t - 1)
    causal = col < row

    def prep(c, _):
        for i in range(PREP_TILES):
            j = c * PREP_TILES + i
            r0 = pl.multiple_of(j * t, t)
            kt = _head_rms(k_ref[pl.ds(r0, t), :], kg_ref[...], hsum_ref[...]).T
            kbd_ref[j, :, :t] = jnp.where(lo_row, kt, 0.0).astype(BF16)
            kbd_ref[j, :, t:] = jnp.where(lo_row, 0.0, kt).astype(BF16)
            v = v_ref[pl.ds(r0, t), :]
            vbd_ref[j, :t, :] = jnp.where(lo_lane, v, 0.0).astype(BF16)
            vbd_ref[j, t:, :] = jnp.where(lo_lane, 0.0, v).astype(BF16)
            qn = _head_rms(q_ref[pl.ds(r0, t), :], qg_ref[...], hsum_ref[...])
            qn_ref[pl.ds(r0, t), :] = qn.astype(BF16)
        return 0

    lax.fori_loop(0, n_ktiles // PREP_TILES, prep, 0)


    def qk(step):
        j, d, _, q0, slot = step
        q = qn_ref[pl.ds(q0 + d * t, Q_TILE - d * t), :]
        z_ref[slot, d * t:, :] = jnp.dot(q, kbd_ref[j], preferred_element_type=F32)

    def scores(step):
        _, d, diag, _, slot = step
        r0 = d * t
        n = Q_TILE - r0
        z = z_ref[slot, r0:, :]
        sp = jnp.log2(1.0 + jnp.exp2(-jnp.abs(z)))
        log_beta = jnp.minimum(z, 0.0) - sp
        log_1mb = log_beta - z
        if diag:
            masked = jnp.where(causal, log_1mb[:t], 0.0)
            log_1mb = masked if n == t else jnp.concatenate([masked, log_1mb[t:]], axis=0)
        lb_ref[slot, r0:, :] = log_beta
        l16 = log_1mb.astype(BF16)
        lst_ref[slot, 0, r0:, :] = l16[:, :t]
        lst_ref[slot, 1, r0:, :] = l16[:, t:]

    def cumsum(step):
        r0 = step[1] * t
        slot = step[4]
        for h in range(2):
            c = h * t
            cum = jnp.dot(lst_ref[slot, h, r0:, :], tri_ref[...], preferred_element_type=F32)
            pre_ref[slot, r0:, c:c + t] = lb_ref[slot, r0:, c:c + t] + cum[:, :t]
            tot_ref[slot, r0:, c:c + t] = cum[:, t:]

    def weights(step):
        j, d, diag, _, slot = step
        r0 = d * t
        n = Q_TILE - r0
        pre_s, tot_s = pre_ref.at[slot], tot_ref.at[slot]
        if diag:
            a = jnp.where(causal, jnp.exp2(pre_s[r0:r0 + t, :]), 0.0)
            carry_ref[r0:r0 + t, :] = tot_s[r0:r0 + t, :]
            if n > t:
                carry = carry_ref[r0 + t:, :]
                a = jnp.concatenate([a, jnp.exp2(pre_s[r0 + t:, :] + carry)], axis=0)
                carry_ref[r0 + t:, :] = carry + tot_s[r0 + t:, :]
        else:
            carry = carry_ref[...]
            a = jnp.exp2(pre_s[...] + carry)
            carry_ref[...] = carry + tot_s[...]
        av = jnp.dot(a.astype(BF16), vbd_ref[j], preferred_element_type=F32)
        if diag:
            acc_ref[r0:r0 + t, :] = av[:t]
            if n > t:
                acc_ref[r0 + t:, :] += av[t:]
        else:
            acc_ref[...] += av

    stages = (qk, scores, cumsum, weights)
    n_stages = len(stages)

    steps = []
    for qi in range(n_qtiles):
        jd = qi * sub
        tile_steps = [(jd + d, d, True, qi * Q_TILE) for d in reversed(range(sub))]
        tile_steps += [(j, 0, False, qi * Q_TILE) for j in reversed(range(jd))]
        steps += [(st + ((len(steps) + i) % 2,), i == len(tile_steps) - 1)
                  for i, st in enumerate(tile_steps)]
    for g in range(len(steps) + n_stages - 1):
        for s in reversed(range(n_stages)):
            if 0 <= g - s < len(steps):
                st, last_of_tile = steps[g - s]
                stages[s](st)
                if last_of_tile and s == n_stages - 1:
                    o_ref[st[3]:st[3] + Q_TILE, :] = acc_ref[...]


def _sb_attn(qkv, q_norm_g, k_norm_g, bsz, seq, cast_weights):
    rows = qkv.shape[0]
    t = ATT_TILE
    n_pairs = SB_DIM // LANES
    n_tiles = seq // t
    n_steps = bsz * n_pairs
    slabs = [w.shape[0] // n_steps for w in cast_weights]
    assert all(w.shape[0] == s * n_steps and s % 16 == 0 for w, s in zip(cast_weights, slabs))
    slab_specs = [pl.BlockSpec((s, w.shape[1]), lambda b, p: (b * n_pairs + p, 0))
                  for w, s in zip(cast_weights, slabs)]
    j_idx = np.arange(t)[:, None]
    s_idx = np.arange(2 * t)[None, :]
    tri = jnp.asarray((s_idx >= t) | (j_idx > s_idx), BF16)
    lane_head = np.arange(LANES) // SB_HEAD_DIM
    hsum = jnp.asarray(np.tile(lane_head, 2)[:, None] == lane_head[None, :], BF16)
    reps = LANES // SB_HEAD_DIM
    qg = (jnp.tile(q_norm_g, reps) * (SB_HEAD_DIM ** -0.5 * LOG2E)).reshape(1, LANES)
    kg = jnp.tile(k_norm_g, reps).reshape(1, LANES)
    return pl.pallas_call(
        functools.partial(_sb_attn_kernel, len(cast_weights)),
        grid=(bsz, n_pairs),
        in_specs=[pl.BlockSpec((seq, LANES), lambda b, p: (b, p)),
                  pl.BlockSpec((seq, LANES), lambda b, p: (b, n_pairs + p)),
                  pl.BlockSpec((seq, LANES), lambda b, p: (b, 2 * n_pairs + p)),
                  _const_spec((1, LANES)),
                  _const_spec((1, LANES)),
                  _const_spec((t, 2 * t)),
                  _const_spec((2 * LANES, LANES))] + slab_specs,
        out_specs=[pl.BlockSpec((seq, LANES), lambda b, p: (b, p))] + slab_specs,
        out_shape=[jax.ShapeDtypeStruct((rows, SB_DIM), F32)]
        + [jax.ShapeDtypeStruct(w.shape, BF16) for w in cast_weights],
        scratch_shapes=[pltpu.VMEM((n_tiles, LANES, 2 * t), BF16),
                        pltpu.VMEM((n_tiles, 2 * t, LANES), BF16),
                        pltpu.VMEM((seq, LANES), BF16),
                        pltpu.VMEM((Q_TILE, LANES), F32),
                        pltpu.VMEM((Q_TILE, 2 * t), F32),
                        pltpu.VMEM((2, Q_TILE, 2 * t), F32),
                        pltpu.VMEM((2, Q_TILE, 2 * t), F32),
                        pltpu.VMEM((2, 2, Q_TILE, t), BF16),
                        pltpu.VMEM((2, Q_TILE, 2 * t), F32),
                        pltpu.VMEM((2, Q_TILE, 2 * t), F32)],
        compiler_params=pltpu.CompilerParams(dimension_semantics=("parallel", "parallel"),
                                             vmem_limit_bytes=VMEM_LIMIT),
        name="sb_attn",
    )(qkv, qkv, qkv, qg, kg, tri, hsum, *cast_weights)


def _mix_ffn_kernel(mix_ref, ya_ref, x_ref, mod_ref, ag_ref, wo_ref, g2_ref, w1_ref, w2_ref,
                    o_ref):
    d_ff = w1_ref.shape[1]
    yb = _rms(ya_ref[...], ag_ref[...]).astype(BF16)
    o = (jnp.dot(mix_ref[...], wo_ref[:CONV_DIM, :], preferred_element_type=F32)
         + jnp.dot(yb, wo_ref[CONV_DIM:, :], preferred_element_type=F32))
    x1 = x_ref[...] + mod_ref[2:3, :] * o
    gain2 = g2_ref[...] * (1.0 + mod_ref[4:5, :])
    h2 = (_rms(x1, gain2) + mod_ref[3:4, :]).astype(BF16)
    f = None
    for c0 in range(0, d_ff, FFN_CHUNK):
        u = jnp.dot(h2, w1_ref[:, c0:c0 + FFN_CHUNK], preferred_element_type=F32)
        r = jnp.maximum(u, 0.0)
        fc = jnp.dot((r * r).astype(BF16), w2_ref[c0:c0 + FFN_CHUNK, :],
                     preferred_element_type=F32)
        f = fc if f is None else f + fc
    o_ref[...] = x1 + mod_ref[5:6, :] * f


def _mix_ffn(mix_a, y_attn, x2, mod3, attn_out_g, w_out_b, norm2_g, w1_b, w2_b, seq):
    rows, d = x2.shape
    tm = ROW_TILE
    tiles_per_seq = seq // tm
    return pl.pallas_call(
        _mix_ffn_kernel,
        grid=(rows // tm,),
        in_specs=[pl.BlockSpec((tm, CONV_DIM), lambda i: (i, 0)),
                  pl.BlockSpec((tm, SB_DIM), lambda i: (i, 0)),
                  pl.BlockSpec((tm, d), lambda i: (i, 0)),
                  pl.BlockSpec((None, N_MOD, d), lambda i: (i // tiles_per_seq, 0, 0)),
                  _const_spec((1, SB_DIM)),
                  _const_spec(w_out_b.shape),
                  _const_spec((1, d)),
                  _const_spec(w1_b.shape),
                  _const_spec(w2_b.shape)],
        out_specs=pl.BlockSpec((tm, d), lambda i: (i, 0)),
        out_shape=jax.ShapeDtypeStruct((rows, d), F32),
        compiler_params=pltpu.CompilerParams(dimension_semantics=("parallel",),
                                             vmem_limit_bytes=VMEM_LIMIT),
        name="mix_ffn",
    )(mix_a, y_attn, x2, mod3, attn_out_g.reshape(1, SB_DIM), w_out_b, norm2_g.reshape(1, d),
      w1_b, w2_b)


def kernel(x, c, w_ada, b_ada, norm1_g, w_in, conv_w, q_norm_g, k_norm_g, conv_out_g,
           attn_out_g, w_out, norm2_g, w_ff1, w_ff2):
    bsz, seq, d = x.shape
    assert seq % IN_ROW_TILE == 0 and seq % ROW_TILE == 0 and seq % Q_TILE == 0
    assert Q_TILE % ATT_TILE == 0 and (seq // ATT_TILE) % PREP_TILES == 0
    assert w_in.shape[1] == 3 * CONV_DIM + 3 * SB_DIM
    x2 = x.reshape(bsz * seq, d)
    mod3 = _adaln(c, w_ada, b_ada).reshape(bsz, N_MOD, d)
    mix_a, qkv = _in_proj(x2, mod3, norm1_g, w_in, conv_w, conv_out_g, seq)
    y_attn, w_out_b, w1_b, w2_b = _sb_attn(qkv, q_norm_g, k_norm_g, bsz, seq,
                                           (w_out, w_ff1, w_ff2))
    out = _mix_ffn(mix_a, y_attn, x2, mod3, attn_out_g, w_out_b, norm2_g, w1_b, w2_b, seq)
    return out.reshape(bsz, seq, d)
```

```python
import functools

import jax
import jax.numpy as jnp
import numpy as np
from jax import lax
from jax.experimental import pallas as pl
from jax.experimental.pallas import tpu as pltpu

EPS = 1e-6
CONV_DIM = 512
SB_DIM = 512
SB_HEAD_DIM = 64
N_MOD = 6
LANES = 128
HALO = 8
ATT_TILE = 128
Q_TILE = 512
PREP_TILES = 16
LOG2E = 1.4426950408889634
ADALN_COLS = 2048
ROW_TILE = 1024
IN_ROW_TILE = 1024
IN_SUB_ROWS = 256
FFN_CHUNK = 512
VMEM_LIMIT = 56 * 1024 * 1024

F32 = jnp.float32
BF16 = jnp.bfloat16


def _split_bf16(a):
    hi = a.astype(BF16)
    lo = (a - hi.astype(F32)).astype(BF16)
    return hi, lo


def _const_spec(shape):
    return pl.BlockSpec(shape, lambda *_: (0,) * len(shape), pipeline_mode=pl.Buffered(1))


def _adaln_kernel(c_ref, w_ref, b_ref, o_ref):
    c = c_ref[...]
    s = c * (1.0 / (1.0 + jnp.exp(-c)))
    s_hi, s_lo = _split_bf16(s)
    w_hi, w_lo = _split_bf16(w_ref[...])
    bsz = c.shape[0]
    dot = functools.partial(jnp.dot, preferred_element_type=F32)
    both = dot(jnp.concatenate([s_hi, s_lo], axis=0), w_hi)
    o_ref[...] = both[:bsz] + (dot(s_hi, w_lo) + both[bsz:]) + b_ref[...]


def _adaln(c, w_ada, b_ada):
    bsz, d = c.shape
    n = w_ada.shape[1]
    tn = ADALN_COLS
    return pl.pallas_call(
        _adaln_kernel,
        grid=(n // tn,),
        in_specs=[pl.BlockSpec((bsz, d), lambda j: (0, 0)),
                  pl.BlockSpec((d, tn), lambda j: (0, j)),
                  pl.BlockSpec((1, tn), lambda j: (0, j))],
        out_specs=pl.BlockSpec((bsz, tn), lambda j: (0, j)),
        out_shape=jax.ShapeDtypeStruct((bsz, n), F32),
        compiler_params=pltpu.CompilerParams(dimension_semantics=("parallel",),
                                             vmem_limit_bytes=VMEM_LIMIT),
        name="adaln",
    )(c, w_ada, b_ada.reshape(1, n))


def _rms(xf, g):
    ms = jnp.mean(xf * xf, axis=-1, keepdims=True)
    return xf * lax.rsqrt(ms + EPS) * g


def _in_proj_kernel(tiles_per_seq, x_ref, mod_ref, g1_ref, w32_ref, cw_ref, cg_ref,
                    mix_ref, qkv_ref, ext_ref, hb_ref, p_ref, w_ref):
    tm = x_ref.shape[0]
    i = pl.program_id(0)

    @pl.when(i == 0)
    def _():
        w_ref[...] = w32_ref[...].astype(BF16)

    @pl.when(i % tiles_per_seq == 0)
    def _():
        ext_ref[0:HALO, :] = jnp.zeros((HALO, CONV_DIM), F32)

    @pl.when(i % tiles_per_seq != 0)
    def _():
        ext_ref[0:HALO, :] = ext_ref[tm:tm + HALO, :]

    shift = mod_ref[0:1, :]
    gain = g1_ref[...] * (1.0 + mod_ref[1:2, :])
    n_conv = 3 * CONV_DIM
    n_sub = tm // IN_SUB_ROWS

    def rows(k):
        return slice(k * IN_SUB_ROWS, (k + 1) * IN_SUB_ROWS)

    def norm(k):
        hb_ref[rows(k), :] = (_rms(x_ref[rows(k), :], gain) + shift).astype(BF16)

    def project(k):
        hb = hb_ref[rows(k), :]
        p_ref[k % 2] = jnp.dot(hb, w_ref[:, :n_conv], preferred_element_type=F32)
        qkv_ref[rows(k), :] = jnp.dot(hb, w_ref[:, n_conv:], preferred_element_type=F32)

    def conv_mixer(k):
        r0 = k * IN_SUB_ROWS
        r1 = r0 + IN_SUB_ROWS
        b_gate = p_ref[k % 2, :, :CONV_DIM]
        cu = p_ref[k % 2, :, CONV_DIM:2 * CONV_DIM] * p_ref[k % 2, :, 2 * CONV_DIM:]
        ext_ref[r0 + HALO:r1 + HALO, :] = cu
        conv = (cw_ref[2:3, :] * cu + cw_ref[1:2, :] * ext_ref[r0 + HALO - 1:r1 + HALO - 1, :]
                + cw_ref[0:1, :] * ext_ref[r0 + HALO - 2:r1 + HALO - 2, :])
        mix_ref[rows(k), :] = _rms(b_gate * conv, cg_ref[...]).astype(BF16)

    norm(0)
    for k in range(n_sub):
        if k + 1 < n_sub:
            norm(k + 1)
        if k >= 1:
            conv_mixer(k - 1)
        project(k)
    conv_mixer(n_sub - 1)


def _in_proj(x2, mod3, norm1_g, w_in, conv_w, conv_out_g, seq):
    rows, d = x2.shape
    tm = IN_ROW_TILE
    tiles_per_seq = seq // tm
    n_in = w_in.shape[1]
    n_qkv = n_in - 3 * CONV_DIM
    return pl.pallas_call(
        functools.partial(_in_proj_kernel, tiles_per_seq),
        grid=(rows // tm,),
        in_specs=[pl.BlockSpec((tm, d), lambda i: (i, 0)),
                  pl.BlockSpec((None, N_MOD, d), lambda i: (i // tiles_per_seq, 0, 0)),
                  _const_spec((1, d)),
                  _const_spec((d, n_in)),
                  _const_spec(conv_w.shape),
                  _const_spec((1, CONV_DIM))],
        out_specs=[pl.BlockSpec((tm, CONV_DIM), lambda i: (i, 0)),
                   pl.BlockSpec((tm, n_qkv), lambda i: (i, 0))],
        out_shape=[jax.ShapeDtypeStruct((rows, CONV_DIM), BF16),
                   jax.ShapeDtypeStruct((rows, n_qkv), F32)],
        scratch_shapes=[pltpu.VMEM((tm + HALO, CONV_DIM), F32),
                        pltpu.VMEM((tm, d), BF16),
                        pltpu.VMEM((2, IN_SUB_ROWS, 3 * CONV_DIM), F32),
                        pltpu.VMEM((d, n_in), BF16)],
        compiler_params=pltpu.CompilerParams(dimension_semantics=("arbitrary",),
                                             vmem_limit_bytes=VMEM_LIMIT),
        name="in_proj",
    )(x2, mod3, norm1_g.reshape(1, d), w_in, conv_w, conv_out_g.reshape(1, CONV_DIM))


def _head_rms(a, g, hsum):
    s_hi, s_lo = _split_bf16(a * a)
    ssq = jnp.dot(jnp.concatenate([s_hi, s_lo], axis=1), hsum, preferred_element_type=F32)
    return a * lax.rsqrt(ssq * (1.0 / SB_HEAD_DIM) + EPS) * g


def _sb_attn_kernel(n_cast, q_ref, k_ref, v_ref, qg_ref, kg_ref, tri_ref, hsum_ref, *refs):
    w32_refs, o_ref, w16_refs = refs[:n_cast], refs[n_cast], refs[n_cast + 1:2 * n_cast + 1]
    (kbd_ref, vbd_ref, qn_ref, acc_ref, carry_ref,
     z_ref, lb_ref, lst_ref, pre_ref, tot_ref) = refs[2 * n_cast + 1:]
    for w32_ref, w16_ref in zip(w32_refs, w16_refs):
        w16_ref[...] = w32_ref[...].astype(BF16)

    t = ATT_TILE
    sub = Q_TILE // t
    n_ktiles = k_ref.shape[0] // t
    n_qtiles = q_ref.shape[0] // Q_TILE
    lo_lane = lax.broadcasted_iota(jnp.int32, (t, LANES), 1) < SB_HEAD_DIM
    lo_row = lax.broadcasted_iota(jnp.int32, (LANES, t), 0) < SB_HEAD_DIM
    row = lax.broadcasted_iota(jnp.int32, (t, 2 * t), 0)
    col = lax.broadcasted_iota(jnp.int32, (t, 2 * t), 1) & (t - 1)
    causal = col < row

    def prep(c, _):
        for i in range(PREP_TILES):
            j = c * PREP_TILES + i
            r0 = pl.multiple_of(j * t, t)
            kt = _head_rms(k_ref[pl.ds(r0, t), :], kg_ref[...], hsum_ref[...]).T
            kbd_ref[j, :, :t] = jnp.where(lo_row, kt, 0.0).astype(BF16)
            kbd_ref[j, :, t:] = jnp.where(lo_row, 0.0, kt).astype(BF16)
            v = v_ref[pl.ds(r0, t), :]
            vbd_ref[j, :t, :] = jnp.where(lo_lane, v, 0.0).astype(BF16)
            vbd_ref[j, t:, :] = jnp.where(lo_lane, 0.0, v).astype(BF16)
            qn = _head_rms(q_ref[pl.ds(r0, t), :], qg_ref[...], hsum_ref[...])
            qn_ref[pl.ds(r0, t), :] = qn.astype(BF16)
        return 0

    lax.fori_loop(0, n_ktiles // PREP_TILES, prep, 0)


    def qk(step):
        j, d, _, q0, slot = step
        q = qn_ref[pl.ds(q0 + d * t, Q_TILE - d * t), :]
        z_ref[slot, d * t:, :] = jnp.dot(q, kbd_ref[j], preferred_element_type=F32)

    def scores(step):
        _, d, diag, _, slot = step
        r0 = d * t
        n = Q_TILE - r0
        z = z_ref[slot, r0:, :]
        sp = jnp.log2(1.0 + jnp.exp2(-jnp.abs(z)))
        log_beta = jnp.minimum(z, 0.0) - sp
        log_1mb = log_beta - z
        if diag:
            masked = jnp.where(causal, log_1mb[:t], 0.0)
            log_1mb = masked if n == t else jnp.concatenate([masked, log_1mb[t:]], axis=0)
        lb_ref[slot, r0:, :] = log_beta
        l16 = log_1mb.astype(BF16)
        lst_ref[slot, 0, r0:, :] = l16[:, :t]
        lst_ref[slot, 1, r0:, :] = l16[:, t:]

    def cumsum(step):
        r0 = step[1] * t
        slot = step[4]
        for h in range(2):
            c = h * t
            cum = jnp.dot(lst_ref[slot, h, r0:, :], tri_ref[...], preferred_element_type=F32)
            pre_ref[slot, r0:, c:c + t] = lb_ref[slot, r0:, c:c + t] + cum[:, :t]
            tot_ref[slot, r0:, c:c + t] = cum[:, t:]

    def weights(step):
        j, d, diag, _, slot = step
        r0 = d * t
        n = Q_TILE - r0
        pre_s, tot_s = pre_ref.at[slot], tot_ref.at[slot]
        if diag:
            a = jnp.where(causal, jnp.exp2(pre_s[r0:r0 + t, :]), 0.0)
            carry_ref[r0:r0 + t, :] = tot_s[r0:r0 + t, :]
            if n > t:
                carry = carry_ref[r0 + t:, :]
                a = jnp.concatenate([a, jnp.exp2(pre_s[r0 + t:, :] + carry)], axis=0)
                carry_ref[r0 + t:, :] = carry + tot_s[r0 + t:, :]
        else:
            carry = carry_ref[...]
            a = jnp.exp2(pre_s[...] + carry)
            carry_ref[...] = carry + tot_s[...]
        av = jnp.dot(a.astype(BF16), vbd_ref[j], preferred_element_type=F32)
        if diag:
            acc_ref[r0:r0 + t, :] = av[:t]
            if n > t:
                acc_ref[r0 + t:, :] += av[t:]
        else:
            acc_ref[...] += av

    stages = (qk, scores, cumsum, weights)
    n_stages = len(stages)

    steps = []
    for qi in range(n_qtiles):
        jd = qi * sub
        tile_steps = [(jd + d, d, True, qi * Q_TILE) for d in reversed(range(sub))]
        tile_steps += [(j, 0, False, qi * Q_TILE) for j in reversed(range(jd))]
        steps += [(st + ((len(steps) + i) % 2,), i == len(tile_steps) - 1)
                  for i, st in enumerate(tile_steps)]
    for g in range(len(steps) + n_stages - 1):
        for s in reversed(range(n_stages)):
            if 0 <= g - s < len(steps):
                st, last_of_tile = steps[g - s]
                stages[s](st)
                if last_of_tile and s == n_stages - 1:
                    o_ref[st[3]:st[3] + Q_TILE, :] = acc_ref[...]


def _sb_attn(qkv, q_norm_g, k_norm_g, bsz, seq, cast_weights):
    rows = qkv.shape[0]
    t = ATT_TILE
    n_pairs = SB_DIM // LANES
    n_tiles = seq // t
    n_steps = bsz * n_pairs
    slabs = [w.shape[0] // n_steps for w in cast_weights]
    assert all(w.shape[0] == s * n_steps and s % 16 == 0 for w, s in zip(cast_weights, slabs))
    slab_specs = [pl.BlockSpec((s, w.shape[1]), lambda b, p: (b * n_pairs + p, 0))
                  for w, s in zip(cast_weights, slabs)]
    j_idx = np.arange(t)[:, None]
    s_idx = np.arange(2 * t)[None, :]
    tri = jnp.asarray((s_idx >= t) | (j_idx > s_idx), BF16)
    lane_head = np.arange(LANES) // SB_HEAD_DIM
    hsum = jnp.asarray(np.tile(lane_head, 2)[:, None] == lane_head[None, :], BF16)
    reps = LANES // SB_HEAD_DIM
    qg = (jnp.tile(q_norm_g, reps) * (SB_HEAD_DIM ** -0.5 * LOG2E)).reshape(1, LANES)
    kg = jnp.tile(k_norm_g, reps).reshape(1, LANES)
    return pl.pallas_call(
        functools.partial(_sb_attn_kernel, len(cast_weights)),
        grid=(bsz, n_pairs),
        in_specs=[pl.BlockSpec((seq, LANES), lambda b, p: (b, p)),
                  pl.BlockSpec((seq, LANES), lambda b, p: (b, n_pairs + p)),
                  pl.BlockSpec((seq, LANES), lambda b, p: (b, 2 * n_pairs + p)),
                  _const_spec((1, LANES)),
                  _const_spec((1, LANES)),
                  _const_spec((t, 2 * t)),
                  _const_spec((2 * LANES, LANES))] + slab_specs,
        out_specs=[pl.BlockSpec((seq, LANES), lambda b, p: (b, p))] + slab_specs,
        out_shape=[jax.ShapeDtypeStruct((rows, SB_DIM), F32)]
        + [jax.ShapeDtypeStruct(w.shape, BF16) for w in cast_weights],
        scratch_shapes=[pltpu.VMEM((n_tiles, LANES, 2 * t), BF16),
                        pltpu.VMEM((n_tiles, 2 * t, LANES), BF16),
                        pltpu.VMEM((seq, LANES), BF16),
                        pltpu.VMEM((Q_TILE, LANES), F32),
                        pltpu.VMEM((Q_TILE, 2 * t), F32),
                        pltpu.VMEM((2, Q_TILE, 2 * t), F32),
                        pltpu.VMEM((2, Q_TILE, 2 * t), F32),
                        pltpu.VMEM((2, 2, Q_TILE, t), BF16),
                        pltpu.VMEM((2, Q_TILE, 2 * t), F32),
                        pltpu.VMEM((2, Q_TILE, 2 * t), F32)],
        compiler_params=pltpu.CompilerParams(dimension_semantics=("parallel", "parallel"),
                                             vmem_limit_bytes=VMEM_LIMIT),
        name="sb_attn",
    )(qkv, qkv, qkv, qg, kg, tri, hsum, *cast_weights)


def _mix_ffn_kernel(mix_ref, ya_ref, x_ref, mod_ref, ag_ref, wo_ref, g2_ref, w1_ref, w2_ref,
                    o_ref):
    d_ff = w1_ref.shape[1]
    yb = _rms(ya_ref[...], ag_ref[...]).astype(BF16)
    o = (jnp.dot(mix_ref[...], wo_ref[:CONV_DIM, :], preferred_element_type=F32)
         + jnp.dot(yb, wo_ref[CONV_DIM:, :], preferred_element_type=F32))
    x1 = x_ref[...] + mod_ref[2:3, :] * o
    gain2 = g2_ref[...] * (1.0 + mod_ref[4:5, :])
    h2 = (_rms(x1, gain2) + mod_ref[3:4, :]).astype(BF16)
    f = None
    for c0 in range(0, d_ff, FFN_CHUNK):
        u = jnp.dot(h2, w1_ref[:, c0:c0 + FFN_CHUNK], preferred_element_type=F32)
        r = jnp.maximum(u, 0.0)
        fc = jnp.dot((r * r).astype(BF16), w2_ref[c0:c0 + FFN_CHUNK, :],
                     preferred_element_type=F32)
        f = fc if f is None else f + fc
    o_ref[...] = x1 + mod_ref[5:6, :] * f


def _mix_ffn(mix_a, y_attn, x2, mod3, attn_out_g, w_out_b, norm2_g, w1_b, w2_b, seq):
    rows, d = x2.shape
    tm = ROW_TILE
    tiles_per_seq = seq // tm
    return pl.pallas_call(
        _mix_ffn_kernel,
        grid=(rows // tm,),
        in_specs=[pl.BlockSpec((tm, CONV_DIM), lambda i: (i, 0)),
                  pl.BlockSpec((tm, SB_DIM), lambda i: (i, 0)),
                  pl.BlockSpec((tm, d), lambda i: (i, 0)),
                  pl.BlockSpec((None, N_MOD, d), lambda i: (i // tiles_per_seq, 0, 0)),
                  _const_spec((1, SB_DIM)),
                  _const_spec(w_out_b.shape),
                  _const_spec((1, d)),
                  _const_spec(w1_b.shape),
                  _const_spec(w2_b.shape)],
        out_specs=pl.BlockSpec((tm, d), lambda i: (i, 0)),
        out_shape=jax.ShapeDtypeStruct((rows, d), F32),
        compiler_params=pltpu.CompilerParams(dimension_semantics=("parallel",),
                                             vmem_limit_bytes=VMEM_LIMIT),
        name="mix_ffn",
    )(mix_a, y_attn, x2, mod3, attn_out_g.reshape(1, SB_DIM), w_out_b, norm2_g.reshape(1, d),
      w1_b, w2_b)


def kernel(x, c, w_ada, b_ada, norm1_g, w_in, conv_w, q_norm_g, k_norm_g, conv_out_g,
           attn_out_g, w_out, norm2_g, w_ff1, w_ff2):
    bsz, seq, d = x.shape
    assert seq % IN_ROW_TILE == 0 and seq % ROW_TILE == 0 and seq % Q_TILE == 0
    assert Q_TILE % ATT_TILE == 0 and (seq // ATT_TILE) % PREP_TILES == 0
    assert w_in.shape[1] == 3 * CONV_DIM + 3 * SB_DIM
    x2 = x.reshape(bsz * seq, d)
    mod3 = _adaln(c, w_ada, b_ada).reshape(bsz, N_MOD, d)
    mix_a, qkv = _in_proj(x2, mod3, norm1_g, w_in, conv_w, conv_out_g, seq)
    y_attn, w_out_b, w1_b, w2_b = _sb_attn(qkv, q_norm_g, k_norm_g, bsz, seq,
                                           (w_out, w_ff1, w_ff2))
    out = _mix_ffn(mix_a, y_attn, x2, mod3, attn_out_g, w_out_b, norm2_g, w1_b, w2_b, seq)
    return out.reshape(bsz, seq, d)
```

```python
import functools

import jax
import jax.numpy as jnp
import numpy as np
from jax import lax
from jax.experimental import pallas as pl
from jax.experimental.pallas import tpu as pltpu

EPS = 1e-6
CONV_DIM = 512
SB_DIM = 512
SB_HEAD_DIM = 64
N_MOD = 6
LANES = 128
HALO = 8
ATT_TILE = 128
Q_TILE = 512
PREP_TILES = 16
LOG2E = 1.4426950408889634
ADALN_COLS = 1024
ROW_TILE = 1024
IN_ROW_TILE = 1024
IN_SUB_ROWS = 256
FFN_CHUNK = 512
VMEM_LIMIT = 56 * 1024 * 1024

F32 = jnp.float32
BF16 = jnp.bfloat16


def _split_bf16(a):
    hi = a.astype(BF16)
    lo = (a - hi.astype(F32)).astype(BF16)
    return hi, lo


def _const_spec(shape):
    return pl.BlockSpec(shape, lambda *_: (0,) * len(shape), pipeline_mode=pl.Buffered(1))


def _adaln_kernel(c_ref, w_ref, b_ref, o_ref):
    c = c_ref[...]
    s = c * (1.0 / (1.0 + jnp.exp(-c)))
    s_hi, s_lo = _split_bf16(s)
    w_hi, w_lo = _split_bf16(w_ref[...])
    bsz = c.shape[0]
    dot = functools.partial(jnp.dot, preferred_element_type=F32)
    both = dot(jnp.concatenate([s_hi, s_lo], axis=0), w_hi)
    o_ref[...] = both[:bsz] + (dot(s_hi, w_lo) + both[bsz:]) + b_ref[...]


def _adaln(c, w_ada, b_ada):
    bsz, d = c.shape
    n = w_ada.shape[1]
    tn = ADALN_COLS
    return pl.pallas_call(
        _adaln_kernel,
        grid=(n // tn,),
        in_specs=[pl.BlockSpec((bsz, d), lambda j: (0, 0)),
                  pl.BlockSpec((d, tn), lambda j: (0, j)),
                  pl.BlockSpec((1, tn), lambda j: (0, j))],
        out_specs=pl.BlockSpec((bsz, tn), lambda j: (0, j)),
        out_shape=jax.ShapeDtypeStruct((bsz, n), F32),
        compiler_params=pltpu.CompilerParams(dimension_semantics=("parallel",),
                                             vmem_limit_bytes=VMEM_LIMIT),
        name="adaln",
    )(c, w_ada, b_ada.reshape(1, n))


def _rms(xf, g):
    ms = jnp.mean(xf * xf, axis=-1, keepdims=True)
    return xf * lax.rsqrt(ms + EPS) * g


def _in_proj_kernel(tiles_per_seq, x_ref, mod_ref, g1_ref, w32_ref, cw_ref, cg_ref,
                    mix_ref, qkv_ref, ext_ref, hb_ref, p_ref, w_ref):
    tm = x_ref.shape[0]
    i = pl.program_id(0)

    @pl.when(i == 0)
    def _():
        w_ref[...] = w32_ref[...].astype(BF16)

    @pl.when(i % tiles_per_seq == 0)
    def _():
        ext_ref[0:HALO, :] = jnp.zeros((HALO, CONV_DIM), F32)

    @pl.when(i % tiles_per_seq != 0)
    def _():
        ext_ref[0:HALO, :] = ext_ref[tm:tm + HALO, :]

    shift = mod_ref[0:1, :]
    gain = g1_ref[...] * (1.0 + mod_ref[1:2, :])
    n_conv = 3 * CONV_DIM
    n_sub = tm // IN_SUB_ROWS

    def rows(k):
        return slice(k * IN_SUB_ROWS, (k + 1) * IN_SUB_ROWS)

    def norm(k):
        hb_ref[rows(k), :] = (_rms(x_ref[rows(k), :], gain) + shift).astype(BF16)

    def project(k):
        hb = hb_ref[rows(k), :]
        p_ref[k % 2] = jnp.dot(hb, w_ref[:, :n_conv], preferred_element_type=F32)
        qkv_ref[rows(k), :] = jnp.dot(hb, w_ref[:, n_conv:], preferred_element_type=F32)

    def conv_mixer(k):
        r0 = k * IN_SUB_ROWS
        r1 = r0 + IN_SUB_ROWS
        b_gate = p_ref[k % 2, :, :CONV_DIM]
        cu = p_ref[k % 2, :, CONV_DIM:2 * CONV_DIM] * p_ref[k % 2, :, 2 * CONV_DIM:]
        ext_ref[r0 + HALO:r1 + HALO, :] = cu
        conv = (cw_ref[2:3, :] * cu + cw_ref[1:2, :] * ext_ref[r0 + HALO - 1:r1 + HALO - 1, :]
                + cw_ref[0:1, :] * ext_ref[r0 + HALO - 2:r1 + HALO - 2, :])
        mix_ref[rows(k), :] = _rms(b_gate * conv, cg_ref[...]).astype(BF16)

    norm(0)
    for k in range(n_sub):
        if k + 1 < n_sub:
            norm(k + 1)
        if k >= 1:
            conv_mixer(k - 1)
        project(k)
    conv_mixer(n_sub - 1)


def _in_proj(x2, mod3, norm1_g, w_in, conv_w, conv_out_g, seq):
    rows, d = x2.shape
    tm = IN_ROW_TILE
    tiles_per_seq = seq // tm
    n_in = w_in.shape[1]
    n_qkv = n_in - 3 * CONV_DIM
    return pl.pallas_call(
        functools.partial(_in_proj_kernel, tiles_per_seq),
        grid=(rows // tm,),
        in_specs=[pl.BlockSpec((tm, d), lambda i: (i, 0)),
                  pl.BlockSpec((None, N_MOD, d), lambda i: (i // tiles_per_seq, 0, 0)),
                  _const_spec((1, d)),
                  _const_spec((d, n_in)),
                  _const_spec(conv_w.shape),
                  _const_spec((1, CONV_DIM))],
        out_specs=[pl.BlockSpec((tm, CONV_DIM), lambda i: (i, 0)),
                   pl.BlockSpec((tm, n_qkv), lambda i: (i, 0))],
        out_shape=[jax.ShapeDtypeStruct((rows, CONV_DIM), BF16),
                   jax.ShapeDtypeStruct((rows, n_qkv), F32)],
        scratch_shapes=[pltpu.VMEM((tm + HALO, CONV_DIM), F32),
                        pltpu.VMEM((tm, d), BF16),
                        pltpu.VMEM((2, IN_SUB_ROWS, 3 * CONV_DIM), F32),
                        pltpu.VMEM((d, n_in), BF16)],
        compiler_params=pltpu.CompilerParams(dimension_semantics=("arbitrary",),
                                             vmem_limit_bytes=VMEM_LIMIT),
        name="in_proj",
    )(x2, mod3, norm1_g.reshape(1, d), w_in, conv_w, conv_out_g.reshape(1, CONV_DIM))


def _head_rms(a, g, hsum):
    ssq = jnp.dot((a * a).astype(BF16), hsum, preferred_element_type=F32)
    return a * lax.rsqrt(ssq * (1.0 / SB_HEAD_DIM) + EPS) * g


def _sb_attn_kernel(n_cast, q_ref, k_ref, v_ref, qg_ref, kg_ref, tri_ref, hsum_ref, *refs):
    w32_refs, o_ref, w16_refs = refs[:n_cast], refs[n_cast], refs[n_cast + 1:2 * n_cast + 1]
    (kbd_ref, vbd_ref, qn_ref, acc_ref, carry_ref,
     z_ref, lb_ref, lst_ref, pre_ref, tot_ref) = refs[2 * n_cast + 1:]
    for w32_ref, w16_ref in zip(w32_refs, w16_refs):
        w16_ref[...] = w32_ref[...].astype(BF16)

    t = ATT_TILE
    sub = Q_TILE // t
    n_ktiles = k_ref.shape[0] // t
    n_qtiles = q_ref.shape[0] // Q_TILE
    lo_lane = lax.broadcasted_iota(jnp.int32, (t, LANES), 1) < SB_HEAD_DIM
    lo_row = lax.broadcasted_iota(jnp.int32, (LANES, t), 0) < SB_HEAD_DIM
    row = lax.broadcasted_iota(jnp.int32, (t, 2 * t), 0)
    col = lax.broadcasted_iota(jnp.int32, (t, 2 * t), 1) & (t - 1)
    causal = col < row

    def prep(c, _):
        for i in range(PREP_TILES):
            j = c * PREP_TILES + i
            r0 = pl.multiple_of(j * t, t)
            kt = _head_rms(k_ref[pl.ds(r0, t), :], kg_ref[...], hsum_ref[...]).T
            kbd_ref[j, :, :t] = jnp.where(lo_row, kt, 0.0).astype(BF16)
            kbd_ref[j, :, t:] = jnp.where(lo_row, 0.0, kt).astype(BF16)
            v = v_ref[pl.ds(r0, t), :]
            vbd_ref[j, :t, :] = jnp.where(lo_lane, v, 0.0).astype(BF16)
            vbd_ref[j, t:, :] = jnp.where(lo_lane, 0.0, v).astype(BF16)
            qn = _head_rms(q_ref[pl.ds(r0, t), :], qg_ref[...], hsum_ref[...])
            qn_ref[pl.ds(r0, t), :] = qn.astype(BF16)
        return 0

    lax.fori_loop(0, n_ktiles // PREP_TILES, prep, 0)


    def qk(step):
        j, d, _, q0, slot = step
        q = qn_ref[pl.ds(q0 + d * t, Q_TILE - d * t), :]
        z_ref[slot, d * t:, :] = jnp.dot(q, kbd_ref[j], preferred_element_type=F32)

    def scores(step):
        _, d, diag, _, slot = step
        r0 = d * t
        n = Q_TILE - r0
        z = z_ref[slot, r0:, :]
        sp = jnp.log2(1.0 + jnp.exp2(-jnp.abs(z)))
        log_beta = jnp.minimum(z, 0.0) - sp
        log_1mb = log_beta - z
        if diag:
            masked = jnp.where(causal, log_1mb[:t], 0.0)
            log_1mb = masked if n == t else jnp.concatenate([masked, log_1mb[t:]], axis=0)
        lb_ref[slot, r0:, :] = log_beta
        l16 = log_1mb.astype(BF16)
        lst_ref[slot, 0, r0:, :] = l16[:, :t]
        lst_ref[slot, 1, r0:, :] = l16[:, t:]

    def cumsum(step):
        r0 = step[1] * t
        slot = step[4]
        for h in range(2):
            c = h * t
            cum = jnp.dot(lst_ref[slot, h, r0:, :], tri_ref[...], preferred_element_type=F32)
            pre_ref[slot, r0:, c:c + t] = lb_ref[slot, r0:, c:c + t] + cum[:, :t]
            tot_ref[slot, r0:, c:c + t] = cum[:, t:]

    def weights(step):
        j, d, diag, _, slot = step
        r0 = d * t
        n = Q_TILE - r0
        pre_s, tot_s = pre_ref.at[slot], tot_ref.at[slot]
        if diag:
            a = jnp.where(causal, jnp.exp2(pre_s[r0:r0 + t, :]), 0.0)
            carry_ref[r0:r0 + t, :] = tot_s[r0:r0 + t, :]
            if n > t:
                carry = carry_ref[r0 + t:, :]
                a = jnp.concatenate([a, jnp.exp2(pre_s[r0 + t:, :] + carry)], axis=0)
                carry_ref[r0 + t:, :] = carry + tot_s[r0 + t:, :]
        else:
            carry = carry_ref[...]
            a = jnp.exp2(pre_s[...] + carry)
            carry_ref[...] = carry + tot_s[...]
        av = jnp.dot(a.astype(BF16), vbd_ref[j], preferred_element_type=F32)
        if diag:
            acc_ref[r0:r0 + t, :] = av[:t]
            if n > t:
                acc_ref[r0 + t:, :] += av[t:]
        else:
            acc_ref[...] += av

    stages = (qk, scores, cumsum, weights)
    n_stages = len(stages)

    steps = []
    for qi in range(n_qtiles):
        jd = qi * sub
        tile_steps = [(jd + d, d, True, qi * Q_TILE) for d in reversed(range(sub))]
        tile_steps += [(j, 0, False, qi * Q_TILE) for j in reversed(range(jd))]
        steps += [(st + ((len(steps) + i) % 2,), i == len(tile_steps) - 1)
                  for i, st in enumerate(tile_steps)]
    for g in range(len(steps) + n_stages - 1):
        for s in reversed(range(n_stages)):
            if 0 <= g - s < len(steps):
                st, last_of_tile = steps[g - s]
                stages[s](st)
                if last_of_tile and s == n_stages - 1:
                    o_ref[st[3]:st[3] + Q_TILE, :] = acc_ref[...]


def _sb_attn(qkv, q_norm_g, k_norm_g, bsz, seq, cast_weights):
    rows = qkv.shape[0]
    t = ATT_TILE
    n_pairs = SB_DIM // LANES
    n_tiles = seq // t
    n_steps = bsz * n_pairs
    slabs = [w.shape[0] // n_steps for w in cast_weights]
    assert all(w.shape[0] == s * n_steps and s % 16 == 0 for w, s in zip(cast_weights, slabs))
    slab_specs = [pl.BlockSpec((s, w.shape[1]), lambda b, p: (b * n_pairs + p, 0))
                  for w, s in zip(cast_weights, slabs)]
    j_idx = np.arange(t)[:, None]
    s_idx = np.arange(2 * t)[None, :]
    tri = jnp.asarray((s_idx >= t) | (j_idx > s_idx), BF16)
    lane_head = np.arange(LANES) // SB_HEAD_DIM
    hsum = jnp.asarray(lane_head[:, None] == lane_head[None, :], BF16)
    reps = LANES // SB_HEAD_DIM
    qg = (jnp.tile(q_norm_g, reps) * (SB_HEAD_DIM ** -0.5 * LOG2E)).reshape(1, LANES)
    kg = jnp.tile(k_norm_g, reps).reshape(1, LANES)
    return pl.pallas_call(
        functools.partial(_sb_attn_kernel, len(cast_weights)),
        grid=(bsz, n_pairs),
        in_specs=[pl.BlockSpec((seq, LANES), lambda b, p: (b, p)),
                  pl.BlockSpec((seq, LANES), lambda b, p: (b, n_pairs + p)),
                  pl.BlockSpec((seq, LANES), lambda b, p: (b, 2 * n_pairs + p)),
                  _const_spec((1, LANES)),
                  _const_spec((1, LANES)),
                  _const_spec((t, 2 * t)),
                  _const_spec((LANES, LANES))] + slab_specs,
        out_specs=[pl.BlockSpec((seq, LANES), lambda b, p: (b, p))] + slab_specs,
        out_shape=[jax.ShapeDtypeStruct((rows, SB_DIM), F32)]
        + [jax.ShapeDtypeStruct(w.shape, BF16) for w in cast_weights],
        scratch_shapes=[pltpu.VMEM((n_tiles, LANES, 2 * t), BF16),
                        pltpu.VMEM((n_tiles, 2 * t, LANES), BF16),
                        pltpu.VMEM((seq, LANES), BF16),
                        pltpu.VMEM((Q_TILE, LANES), F32),
                        pltpu.VMEM((Q_TILE, 2 * t), F32),
                        pltpu.VMEM((2, Q_TILE, 2 * t), F32),
                        pltpu.VMEM((2, Q_TILE, 2 * t), F32),
                        pltpu.VMEM((2, 2, Q_TILE, t), BF16),
                        pltpu.VMEM((2, Q_TILE, 2 * t), F32),
                        pltpu.VMEM((2, Q_TILE, 2 * t), F32)],
        compiler_params=pltpu.CompilerParams(dimension_semantics=("parallel", "parallel"),
                                             vmem_limit_bytes=VMEM_LIMIT),
        name="sb_attn",
    )(qkv, qkv, qkv, qg, kg, tri, hsum, *cast_weights)


def _mix_ffn_kernel(mix_ref, ya_ref, x_ref, mod_ref, ag_ref, wo_ref, g2_ref, w1_ref, w2_ref,
                    o_ref):
    d_ff = w1_ref.shape[1]
    yb = _rms(ya_ref[...], ag_ref[...]).astype(BF16)
    o = (jnp.dot(mix_ref[...], wo_ref[:CONV_DIM, :], preferred_element_type=F32)
         + jnp.dot(yb, wo_ref[CONV_DIM:, :], preferred_element_type=F32))
    x1 = x_ref[...] + mod_ref[2:3, :] * o
    gain2 = g2_ref[...] * (1.0 + mod_ref[4:5, :])
    h2 = (_rms(x1, gain2) + mod_ref[3:4, :]).astype(BF16)
    f = None
    for c0 in range(0, d_ff, FFN_CHUNK):
        u = jnp.dot(h2, w1_ref[:, c0:c0 + FFN_CHUNK], preferred_element_type=F32)
        r = jnp.maximum(u, 0.0)
        fc = jnp.dot((r * r).astype(BF16), w2_ref[c0:c0 + FFN_CHUNK, :],
                     preferred_element_type=F32)
        f = fc if f is None else f + fc
    o_ref[...] = x1 + mod_ref[5:6, :] * f


def _mix_ffn(mix_a, y_attn, x2, mod3, attn_out_g, w_out_b, norm2_g, w1_b, w2_b, seq):
    rows, d = x2.shape
    tm = ROW_TILE
    tiles_per_seq = seq // tm
    return pl.pallas_call(
        _mix_ffn_kernel,
        grid=(rows // tm,),
        in_specs=[pl.BlockSpec((tm, CONV_DIM), lambda i: (i, 0)),
                  pl.BlockSpec((tm, SB_DIM), lambda i: (i, 0)),
                  pl.BlockSpec((tm, d), lambda i: (i, 0)),
                  pl.BlockSpec((None, N_MOD, d), lambda i: (i // tiles_per_seq, 0, 0)),
                  _const_spec((1, SB_DIM)),
                  _const_spec(w_out_b.shape),
                  _const_spec((1, d)),
                  _const_spec(w1_b.shape),
                  _const_spec(w2_b.shape)],
        out_specs=pl.BlockSpec((tm, d), lambda i: (i, 0)),
        out_shape=jax.ShapeDtypeStruct((rows, d), F32),
        compiler_params=pltpu.CompilerParams(dimension_semantics=("parallel",),
                                             vmem_limit_bytes=VMEM_LIMIT),
        name="mix_ffn",
    )(mix_a, y_attn, x2, mod3, attn_out_g.reshape(1, SB_DIM), w_out_b, norm2_g.reshape(1, d),
      w1_b, w2_b)


def kernel(x, c, w_ada, b_ada, norm1_g, w_in, conv_w, q_norm_g, k_norm_g, conv_out_g,
           attn_out_g, w_out, norm2_g, w_ff1, w_ff2):
    bsz, seq, d = x.shape
    assert seq % IN_ROW_TILE == 0 and seq % ROW_TILE == 0 and seq % Q_TILE == 0
    assert Q_TILE % ATT_TILE == 0 and (seq // ATT_TILE) % PREP_TILES == 0
    assert w_in.shape[1] == 3 * CONV_DIM + 3 * SB_DIM
    x2 = x.reshape(bsz * seq, d)
    mod3 = _adaln(c, w_ada, b_ada).reshape(bsz, N_MOD, d)
    mix_a, qkv = _in_proj(x2, mod3, norm1_g, w_in, conv_w, conv_out_g, seq)
    y_attn, w_out_b, w1_b, w2_b = _sb_attn(qkv, q_norm_g, k_norm_g, bsz, seq,
                                           (w_out, w_ff1, w_ff2))
    out = _mix_ffn(mix_a, y_attn, x2, mod3, attn_out_g, w_out_b, norm2_g, w1_b, w2_b, seq)
    return out.reshape(bsz, seq, d)
```
